```python
import math
import jax, jax.numpy as jnp
from jax import lax
import numpy as np

D_MODEL = 2048
BATCH = 4
SEQ = 2048
DEPTH = 1
DEC_BATCH = 128
DEC_SEQ = 4
PAST_LEN = 16384
PAGE_SIZE = 128

D_RNN = D_MODEL // 2
RNN_BLOCKS = 16
RNN_BW = D_RNN // RNN_BLOCKS
CONV_W = 4
LRU_C = 8.0
D_GMLP = D_MODEL // 2
CHUNK = 128
GMLP_GROUPS = 8
GMLP_GW = D_GMLP // GMLP_GROUPS
D_FF = 5632
ALPHA = (2.0 * DEPTH) ** 0.25
BETA = (8.0 * DEPTH) ** -0.25
LN_EPS = 1e-5
N_MOD = 9
IN_COLS = 2 * D_RNN + 2 * D_GMLP + 2 * D_MODEL

kernel_name = "hybrid_rglru_chunkgmlp_macaron_decode_step"


def layer_norm(x, g, b):
    xf = x.astype(jnp.float32)
    mu = jnp.mean(xf, axis=-1, keepdims=True)
    var = jnp.mean(jnp.square(xf - mu), axis=-1, keepdims=True)
    y = (xf - mu) * lax.rsqrt(var + LN_EPS)
    return (y * g.astype(jnp.float32) + b.astype(jnp.float32)).astype(x.dtype)


def swiglu(u, w_gu, w_down):
    gv = u @ w_gu
    g, v = jnp.split(gv, 2, axis=-1)
    return (jax.nn.silu(g) * v) @ w_down


def causal_conv(xpad, w, b):
    T = xpad.shape[1] - CONV_W + 1
    out = b
    for k in range(CONV_W):
        out = out + xpad[:, k:k + T] * w[k]
    return out


def rg_lru(x, h0, wa, ba, wx, bx, lam, reset_first):
    B, T, _ = x.shape
    f32 = jnp.float32
    xf = x.astype(f32)
    xb = xf.reshape(B, T, RNN_BLOCKS, RNN_BW)
    r = jax.nn.sigmoid(jnp.einsum('btnd,nde->btne', xb, wa.astype(f32)).reshape(B, T, D_RNN) + ba.astype(f32))
    i = jax.nn.sigmoid(jnp.einsum('btnd,nde->btne', xb, wx.astype(f32)).reshape(B, T, D_RNN) + bx.astype(f32))
    log_a = -LRU_C * r * jax.nn.softplus(-lam.astype(f32))
    a = jnp.exp(log_a)
    mult = jnp.sqrt(-jnp.expm1(2.0 * log_a))
    if reset_first:
        mult = jnp.where((jnp.arange(T) == 0)[None, :, None], 1.0, mult)
    bt = mult * i * xf

    def step(h, ab):
        a_t, b_t = ab
        h = a_t * h + b_t
        return h, h

    hT, ys = lax.scan(step, h0.astype(f32), (jnp.swapaxes(a, 0, 1), jnp.swapaxes(bt, 0, 1)))
    return jnp.swapaxes(ys, 0, 1), hT


def chunk_spatial_mix(v, w_s, b_s):
    B, T, _ = v.shape
    n = min(T, CHUNK)
    mask = jnp.tril(jnp.ones((n, n), dtype=bool))
    w = jnp.where(mask[None], w_s[:, :n, :n], 0.0)
    vb = v.reshape(B, T // n, n, GMLP_GROUPS, GMLP_GW)
    s = jnp.einsum('gts,bcsgd->bctgd', w, vb) + jnp.transpose(b_s[:, :n])[None, None, :, :, None]
    return s.reshape(B, T, D_GMLP)


def token_mixing(u, conv_buf, h0, reset_first, p):
    proj = u @ p['w_in']
    o1 = D_RNN
    o2 = o1 + D_RNN
    o3 = o2 + D_GMLP
    o4 = o3 + D_GMLP
    o5 = o4 + D_MODEL
    xr, gr, gu, gv, ga, gb = (proj[..., :o1], proj[..., o1:o2], proj[..., o2:o3],
                              proj[..., o3:o4], proj[..., o4:o5], proj[..., o5:])
    xpad = jnp.concatenate([conv_buf.astype(xr.dtype), xr], axis=1)
    new_buf = xpad[:, -(CONV_W - 1):]
    xc = causal_conv(xpad, p['conv_w'], p['conv_b'])
    y_lru, hT = rg_lru(xc, h0, p['lru_wa'], p['lru_ba'], p['lru_wx'], p['lru_bx'], p['lru_lambda'], reset_first)
    y_a = (y_lru.astype(u.dtype) * jax.nn.gelu(gr)) @ p['w_pa']
    vn = layer_norm(gv, p['gmlp_ln_g'], p['gmlp_ln_b'])
    s = chunk_spatial_mix(vn, p['gmlp_ws'], p['gmlp_bs'])
    y_b = (gu * s) @ p['w_pb']
    m = jax.nn.sigmoid(ga) * y_a + jax.nn.sigmoid(gb) * y_b
    return m @ p['w_out'], new_buf, hT.astype(u.dtype), vn


def decoder_layer(x, c, conv_buf, h0, reset_first, p):
    mod = jax.nn.silu(c) @ p['w_ada'] + p['b_ada']
    mod = mod.reshape(c.shape[0], 1, N_MOD, D_MODEL)
    sh1, sc1, g1, sh2, sc2, g2, sh3, sc3, g3 = [mod[:, :, k] for k in range(N_MOD)]
    ln_g, ln_b = p['ln_g'], p['ln_b']
    u = x * (1.0 + sc1) + sh1
    x = layer_norm(ALPHA * x + 0.5 * g1 * swiglu(u, p['ffn1_w_gu'], p['ffn1_w_down']), ln_g[0], ln_b[0])
    u = x * (1.0 + sc2) + sh2
    mix, new_buf, hT, vn = token_mixing(u, conv_buf, h0, reset_first, p)
    x = layer_norm(ALPHA * x + g2 * mix, ln_g[1], ln_b[1])
    u = x * (1.0 + sc3) + sh3
    x = layer_norm(ALPHA * x + 0.5 * g3 * swiglu(u, p['ffn2_w_gu'], p['ffn2_w_down']), ln_g[2], ln_b[2])
    return x, new_buf, hT, vn


def setup_inputs(seed: int = 0) -> dict:
    key = jax.random.key(seed)
    ks = jax.random.split(key, 32)
    f32 = jnp.float32
    L = DEPTH

    def nrm(k, shape, scale):
        return jax.random.normal(k, shape, f32) * scale

    a8 = jax.random.uniform(ks[20], (L, D_RNN), f32, 0.9, 0.999)
    a = a8 ** (1.0 / LRU_C)
    lam = jnp.log(a) - jnp.log1p(-a)
    return {
        'x_prompt': nrm(ks[0], (BATCH, SEQ, D_MODEL), 1.0),
        'x_sample': nrm(ks[1], (DEC_BATCH, DEC_SEQ, D_MODEL), 1.0),
        'state_conv': nrm(ks[2], (L, DEC_BATCH, CONV_W - 1, D_RNN), 1.0),
        'state_h': nrm(ks[3], (L, DEC_BATCH, D_RNN), 0.5),
        'c_prompt': nrm(ks[4], (BATCH, D_MODEL), 1.0),
        'c_sample': nrm(ks[5], (DEC_BATCH, D_MODEL), 1.0),
        'w_ada': nrm(ks[6], (L, D_MODEL, N_MOD * D_MODEL), 0.5 * D_MODEL ** -0.5),
        'b_ada': nrm(ks[7], (L, N_MOD * D_MODEL), 0.02),
        'ffn1_w_gu': nrm(ks[8], (L, D_MODEL, 2 * D_FF), D_MODEL ** -0.5),
        'ffn1_w_down': nrm(ks[9], (L, D_FF, D_MODEL), BETA * D_FF ** -0.5),
        'ffn2_w_gu': nrm(ks[10], (L, D_MODEL, 2 * D_FF), D_MODEL ** -0.5),
        'ffn2_w_down': nrm(ks[11], (L, D_FF, D_MODEL), BETA * D_FF ** -0.5),
        'w_in': nrm(ks[12], (L, D_MODEL, IN_COLS), D_MODEL ** -0.5),
        'conv_w': nrm(ks[13], (L, CONV_W, D_RNN), CONV_W ** -0.5),
        'conv_b': nrm(ks[14], (L, D_RNN), 0.02),
        'lru_wa': nrm(ks[15], (L, RNN_BLOCKS, RNN_BW, RNN_BW), RNN_BW ** -0.5),
        'lru_ba': nrm(ks[16], (L, D_RNN), 0.02),
        'lru_wx': nrm(ks[17], (L, RNN_BLOCKS, RNN_BW, RNN_BW), RNN_BW ** -0.5),
        'lru_bx': nrm(ks[18], (L, D_RNN), 0.02),
        'lru_lambda': lam,
        'gmlp_ln_g': 1.0 + nrm(ks[21], (L, D_GMLP), 0.02),
        'gmlp_ln_b': nrm(ks[22], (L, D_GMLP), 0.02),
        'gmlp_ws': nrm(ks[23], (L, GMLP_GROUPS, CHUNK, CHUNK), CHUNK ** -0.5),
        'gmlp_bs': 1.0 + nrm(ks[24], (L, GMLP_GROUPS, CHUNK), 0.1),
        'w_pa': nrm(ks[25], (L, D_RNN, D_MODEL), D_RNN ** -0.5),
        'w_pb': nrm(ks[26], (L, D_GMLP, D_MODEL), D_GMLP ** -0.5),
        'w_out': nrm(ks[27], (L, D_MODEL, D_MODEL), BETA * D_MODEL ** -0.5),
        'ln_g': 1.0 + nrm(ks[28], (L, 3, D_MODEL), 0.02),
        'ln_b': nrm(ks[29], (L, 3, D_MODEL), 0.02),
    }


def reference(x_prompt, x_sample, state_conv, state_h, c_prompt, c_sample,
              w_ada, b_ada, ffn1_w_gu, ffn1_w_down, ffn2_w_gu, ffn2_w_down,
              w_in, conv_w, conv_b, lru_wa, lru_ba, lru_wx, lru_bx, lru_lambda,
              gmlp_ln_g, gmlp_ln_b, gmlp_ws, gmlp_bs, w_pa, w_pb, w_out, ln_g, ln_b):
    yp, ys = x_prompt, x_sample
    bp = x_prompt.shape[0]
    conv_p, h_p, conv_s, h_s, v_s = [], [], [], [], []
    for l in range(DEPTH):
        p = {
            'w_ada': w_ada[l], 'b_ada': b_ada[l],
            'ffn1_w_gu': ffn1_w_gu[l], 'ffn1_w_down': ffn1_w_down[l],
            'ffn2_w_gu': ffn2_w_gu[l], 'ffn2_w_down': ffn2_w_down[l],
            'w_in': w_in[l], 'conv_w': conv_w[l], 'conv_b': conv_b[l],
            'lru_wa': lru_wa[l], 'lru_ba': lru_ba[l], 'lru_wx': lru_wx[l], 'lru_bx': lru_bx[l],
            'lru_lambda': lru_lambda[l],
            'gmlp_ln_g': gmlp_ln_g[l], 'gmlp_ln_b': gmlp_ln_b[l],
            'gmlp_ws': gmlp_ws[l], 'gmlp_bs': gmlp_bs[l],
            'w_pa': w_pa[l], 'w_pb': w_pb[l], 'w_out': w_out[l],
            'ln_g': ln_g[l], 'ln_b': ln_b[l],
        }
        zero_buf = jnp.zeros((bp, CONV_W - 1, D_RNN), x_prompt.dtype)
        zero_h = jnp.zeros((bp, D_RNN), jnp.float32)
        yp, cbp, hbp, _ = decoder_layer(yp, c_prompt, zero_buf, zero_h, True, p)
        ys, cbs, hbs, vns = decoder_layer(ys, c_sample, state_conv[l], state_h[l], False, p)
        conv_p.append(cbp)
        h_p.append(hbp)
        conv_s.append(cbs)
        h_s.append(hbs)
        v_s.append(vns)
    return (yp, ys, jnp.stack(conv_p), jnp.stack(h_p), jnp.stack(conv_s), jnp.stack(h_s), jnp.stack(v_s))
```

```python
import functools
import math

import jax
import jax.numpy as jnp
from jax import lax
from jax.experimental import pallas as pl
from jax.experimental.pallas import tpu as pltpu

D_MODEL = 2048
D_RNN = 1024
RNN_BLOCKS = 16
RNN_BW = D_RNN // RNN_BLOCKS
CONV_W = 4
LRU_C = 8.0
D_GMLP = 1024
CHUNK = 128
GMLP_GROUPS = 8
GMLP_GW = D_GMLP // GMLP_GROUPS
D_FF = 5632
N_MOD = 9
IN_COLS = 2 * D_RNN + 2 * D_GMLP + 2 * D_MODEL
LN_EPS = 1e-5
DEPTH = 1
ALPHA = (2.0 * DEPTH) ** 0.25

V7X_SUBLANES = 8
V7X_LANES = 128
V7X_MXU_DIM = 256
V7X_VMEM_LIMIT_BYTES = 56 * 1024 * 1024

BF16 = jnp.bfloat16
F32 = jnp.float32


def _dot(a, b):
    return jnp.dot(a, b, preferred_element_type=F32)


def _layer_norm(x, g, b):
    mu = jnp.mean(x, axis=-1, keepdims=True)
    xc = x - mu
    var = jnp.mean(xc * xc, axis=-1, keepdims=True)
    return xc * lax.rsqrt(var + LN_EPS) * g + b


def _gelu_tanh(x):
    c = math.sqrt(2.0 / math.pi)
    return 0.5 * x * (1.0 + jnp.tanh(c * (x + 0.044715 * (x * x * x))))


def _softplus(x):
    return jnp.maximum(x, 0.0) + jnp.log1p(jnp.exp(-jnp.abs(x)))


def _params(*semantics):
    return pltpu.CompilerParams(dimension_semantics=semantics,
                                vmem_limit_bytes=V7X_VMEM_LIMIT_BYTES)


def _slabs(tm, mod_rows):
    if mod_rows == 1:
        return [(0, tm)]
    return [(s, mod_rows) for s in range(0, tm, mod_rows)]


def _mod_kernel(c_ref, w_ref, b_ref, o_ref):
    c = c_ref[...]
    a = (c * jax.nn.sigmoid(c)).astype(BF16)
    o_ref[...] = _dot(a, w_ref[...]) + b_ref[...]


def _modulation(c_all, w_ada, b_ada, tn=1024):
    rows = c_all.shape[0]
    n = w_ada.shape[1]
    return pl.pallas_call(
        _mod_kernel,
        grid=(n // tn,),
        in_specs=[
            pl.BlockSpec((rows, D_MODEL), lambda j: (0, 0)),
            pl.BlockSpec((D_MODEL, tn), lambda j: (0, j)),
            pl.BlockSpec((1, tn), lambda j: (0, j)),
        ],
        out_specs=pl.BlockSpec((rows, tn), lambda j: (0, j)),
        out_shape=jax.ShapeDtypeStruct((rows, n), F32),
        compiler_params=_params("arbitrary"),
        name="adaln_mod",
    )(c_all, w_ada, b_ada)


def _ffn_kernel(x_ref, mod_ref, wg_ref, wv_ref, wd_ref, lng_ref, lnb_ref, o_ref,
                u_ref, acc_ref, *, slabs, n_f):
    j = pl.program_id(1)

    @pl.when(j == 0)
    def _():
        for r0, nr in slabs:
            rows = pl.ds(r0, nr)
            x = x_ref[rows, :]
            u_ref[rows, :] = (x * (1.0 + mod_ref[0, 1]) + mod_ref[0, 0]).astype(BF16)
        acc_ref[...] = jnp.zeros_like(acc_ref)

    u = u_ref[...]
    g = _dot(u, wg_ref[...])
    v = _dot(u, wv_ref[...])
    h = (g * jax.nn.sigmoid(g) * v).astype(BF16)
    acc_ref[...] += _dot(h, wd_ref[...])

    @pl.when(j == n_f - 1)
    def _():
        for r0, nr in slabs:
            rows = pl.ds(r0, nr)
            y = ALPHA * x_ref[rows, :] + 0.5 * mod_ref[0, 2] * acc_ref[rows, :]
            o_ref[rows, :] = _layer_norm(y, lng_ref[...], lnb_ref[...])


def _ffn(x, mod, w_gu, w_down, ln_g, ln_b, *, tm, tf, rows_per_mod):
    rows = x.shape[0]
    mod_rows = mod.shape[2]
    n_f = D_FF // tf
    kern = functools.partial(_ffn_kernel, slabs=_slabs(tm, mod_rows), n_f=n_f)
    tiles_per_mod = rows_per_mod // tm
    return pl.pallas_call(
        kern,
        grid=(rows // tm, n_f),
        in_specs=[
            pl.BlockSpec((tm, D_MODEL), lambda i, j: (i, 0)),
            pl.BlockSpec((1, 3, mod_rows, D_MODEL), lambda i, j: (i // tiles_per_mod, 0, 0, 0)),
            pl.BlockSpec((D_MODEL, tf), lambda i, j: (0, j)),
            pl.BlockSpec((D_MODEL, tf), lambda i, j: (0, n_f + j)),
            pl.BlockSpec((tf, D_MODEL), lambda i, j: (j, 0)),
            pl.BlockSpec((1, D_MODEL), lambda i, j: (0, 0)),
            pl.BlockSpec((1, D_MODEL), lambda i, j: (0, 0)),
        ],
        out_specs=pl.BlockSpec((tm, D_MODEL), lambda i, j: (i, 0)),
        out_shape=jax.ShapeDtypeStruct((rows, D_MODEL), F32),
        scratch_shapes=[pltpu.VMEM((tm, D_MODEL), BF16), pltpu.VMEM((tm, D_MODEL), F32)],
        compiler_params=_params("arbitrary", "arbitrary"),
        name="macaron_ffn",
    )(x, mod, w_gu, w_gu, w_down, ln_g, ln_b)


_PROJ_TN = 1024


def _proj_kernel(x_ref, mod_ref, w_ref, lng_ref, lnb_ref, o_ref, u_ref, *, slabs):
    j = pl.program_id(1)

    @pl.when(j == 0)
    def _():
        for r0, nr in slabs:
            rows = pl.ds(r0, nr)
            u_ref[rows, :] = (x_ref[rows, :] * (1.0 + mod_ref[0, 1]) + mod_ref[0, 0]).astype(BF16)

    r = _dot(u_ref[...], w_ref[...])

    @pl.when(jnp.logical_or(j == 0, j == 2))
    def _():
        o_ref[...] = r

    @pl.when(j == 1)
    def _():
        o_ref[...] = _gelu_tanh(r)

    @pl.when(j == 3)
    def _():
        o_ref[...] = _layer_norm(r, lng_ref[...], lnb_ref[...])

    @pl.when(j >= 4)
    def _():
        o_ref[...] = jax.nn.sigmoid(r)


def _in_proj(x, mod, w_in, gln_g, gln_b, *, tm, rows_per_mod):
    rows = x.shape[0]
    mod_rows = mod.shape[2]
    kern = functools.partial(_proj_kernel, slabs=_slabs(tm, mod_rows))
    tiles_per_mod = rows_per_mod // tm
    return pl.pallas_call(
        kern,
        grid=(rows // tm, IN_COLS // _PROJ_TN),
        in_specs=[
            pl.BlockSpec((tm, D_MODEL), lambda i, j: (i, 0)),
            pl.BlockSpec((1, 3, mod_rows, D_MODEL), lambda i, j: (i // tiles_per_mod, 0, 0, 0)),
            pl.BlockSpec((D_MODEL, _PROJ_TN), lambda i, j: (0, j)),
            pl.BlockSpec((1, D_GMLP), lambda i, j: (0, 0)),
            pl.BlockSpec((1, D_GMLP), lambda i, j: (0, 0)),
        ],
        out_specs=pl.BlockSpec((tm, _PROJ_TN), lambda i, j: (i, j)),
        out_shape=jax.ShapeDtypeStruct((rows, IN_COLS), F32),
        scratch_shapes=[pltpu.VMEM((tm, D_MODEL), BF16)],
        compiler_params=_params("arbitrary", "arbitrary"),
        name="in_proj",
    )(x, mod, w_in, gln_g, gln_b)


def _lru_coeffs(xc, wbd_ref, ba, bx, lam):
    xcb = xc.astype(BF16)
    ra, rx = [], []
    for q in range(D_RNN // V7X_MXU_DIM):
        cols = slice(q * V7X_MXU_DIM, (q + 1) * V7X_MXU_DIM)
        rq = _dot(xcb[:, cols], wbd_ref[q])
        ra.append(rq[:, :V7X_MXU_DIM])
        rx.append(rq[:, V7X_MXU_DIM:])
    r = jax.nn.sigmoid(jnp.concatenate(ra, axis=-1) + ba)
    i = jax.nn.sigmoid(jnp.concatenate(rx, axis=-1) + bx)
    log_a = (-LRU_C * r) * _softplus(-lam)
    a = jnp.exp(log_a)
    mult = jnp.sqrt(-jnp.tanh(log_a) * (a * a + 1.0))
    return a, mult, i * xc


def _prompt_lru_kernel(xr_ref, gg_ref, cw_ref, cb_ref, wbd_ref, ba_ref, bx_ref, lam_ref,
                       ya_ref, ht_ref, tail_ref, h_ref, *, tt):
    t = pl.program_id(1)

    @pl.when(t == 0)
    def _():
        tail_ref[...] = jnp.zeros_like(tail_ref)
        h_ref[...] = jnp.zeros_like(h_ref)

    xr = xr_ref[...]
    prev = tail_ref[...]
    row = lax.broadcasted_iota(jnp.int32, (tt, D_RNN), 0)
    row8 = lax.broadcasted_iota(jnp.int32, (V7X_SUBLANES, D_RNN), 0)

    xc = cb_ref[...]
    for k in range(CONV_W):
        s = CONV_W - 1 - k
        if s == 0:
            xs = xr
        else:
            rolled = pltpu.roll(xr, s, 0)
            head = jnp.where(row8 < s, pltpu.roll(prev, s, 0), rolled[:V7X_SUBLANES])
            xs = jnp.concatenate([head, rolled[V7X_SUBLANES:]], axis=0)
        xc = xc + xs * cw_ref[k:k + 1, :]
    tail_ref[...] = xr[tt - V7X_SUBLANES:, :]

    a, mult, gated = _lru_coeffs(xc, wbd_ref, ba_ref[...], bx_ref[...], lam_ref[...])
    mult = jnp.where(jnp.logical_and(row == 0, t == 0), 1.0, mult)
    b = mult * gated

    s = 1
    while s < tt:
        keep = row >= s
        a_prev = jnp.where(keep, pltpu.roll(a, s, 0), 1.0)
        b_prev = jnp.where(keep, pltpu.roll(b, s, 0), 0.0)
        b = a * b_prev + b
        a = a * a_prev
        s *= 2
    h = a * h_ref[...] + b
    h_last = h[tt - 1:tt, :]
    h_ref[...] = h_last
    ht_ref[0] = h_last
    ya_ref[...] = (h * gg_ref[...]).astype(BF16)


def _prompt_lru(proj, conv_w, conv_b, wbd, ba, bx, lam, *, batch, seq, tt):
    nt = seq // tt
    kern = functools.partial(_prompt_lru_kernel, tt=tt)
    vec = pl.BlockSpec((1, D_RNN), lambda b, t: (0, 0))
    return pl.pallas_call(
        kern,
        grid=(batch, nt),
        in_specs=[
            pl.BlockSpec((tt, D_RNN), lambda b, t: (b * nt + t, 0)),
            pl.BlockSpec((tt, D_RNN), lambda b, t: (b * nt + t, 1)),
            pl.BlockSpec((CONV_W, D_RNN), lambda b, t: (0, 0)),
            vec,
            pl.BlockSpec((D_RNN // V7X_MXU_DIM, V7X_MXU_DIM, 2 * V7X_MXU_DIM), lambda b, t: (0, 0, 0)),
            vec, vec, vec,
        ],
        out_specs=[
            pl.BlockSpec((tt, D_RNN), lambda b, t: (b * nt + t, 0)),
            pl.BlockSpec((1, 1, D_RNN), lambda b, t: (b, 0, 0)),
        ],
        out_shape=[
            jax.ShapeDtypeStruct((batch * seq, D_RNN), BF16),
            jax.ShapeDtypeStruct((batch, 1, D_RNN), F32),
        ],
        scratch_shapes=[pltpu.VMEM((V7X_SUBLANES, D_RNN), F32), pltpu.VMEM((1, D_RNN), F32)],
        compiler_params=_params("arbitrary", "arbitrary"),
        name="prompt_rglru",
    )(proj, proj, conv_w, conv_b, wbd, ba, bx, lam)


def _prompt_gmlp_kernel(gu_ref, vn_ref, ws_ref, bs_ref, yb_ref, *, tm):
    ti = lax.broadcasted_iota(jnp.int32, (CHUNK, CHUNK), 0)
    si = lax.broadcasted_iota(jnp.int32, (CHUNK, CHUNK), 1)
    causal = si <= ti
    for g in range(GMLP_GROUPS):
        w = jnp.where(causal, ws_ref[g], 0.0).astype(BF16)
        cols = pl.ds(g * GMLP_GW, GMLP_GW)
        bias = bs_ref[:, cols]
        for c in range(tm // CHUNK):
            rows = pl.ds(c * CHUNK, CHUNK)
            s = _dot(w, vn_ref[rows, cols].astype(BF16)) + bias
            yb_ref[rows, cols] = (gu_ref[rows, cols] * s).astype(BF16)


def _prompt_gmlp(proj, ws, bs_tile, *, tm):
    rows = proj.shape[0]
    kern = functools.partial(_prompt_gmlp_kernel, tm=tm)
    return pl.pallas_call(
        kern,
        grid=(rows // tm,),
        in_specs=[
            pl.BlockSpec((tm, D_GMLP), lambda i: (i, 2)),
            pl.BlockSpec((tm, D_GMLP), lambda i: (i, 3)),
            pl.BlockSpec((GMLP_GROUPS, CHUNK, CHUNK), lambda i: (0, 0, 0)),
            pl.BlockSpec((CHUNK, D_GMLP), lambda i: (0, 0)),
        ],
        out_specs=pl.BlockSpec((tm, D_GMLP), lambda i: (i, 0)),
        out_shape=jax.ShapeDtypeStruct((rows, D_GMLP), BF16),
        compiler_params=_params("arbitrary"),
        name="prompt_gmlp",
    )(proj, proj, ws, bs_tile)


def _sample_mix_kernel(xr_ref, gg_ref, gu_ref, vn_ref, sconv_ref, h0_ref, cw_ref, cb_ref,
                       wbd_ref, ba_ref, bx_ref, lam_ref, wsx_ref, bsx_ref,
                       ya_ref, yb_ref, ht_ref, xc_ref, *, nb, nt):
    def slab(t):
        return pl.ds(t * nb, nb)

    xpad = [sconv_ref[k] for k in range(CONV_W - 1)] + [xr_ref[slab(t), :] for t in range(nt)]
    for t in range(nt):
        xc = cb_ref[...]
        for k in range(CONV_W):
            xc = xc + xpad[t + k] * cw_ref[k:k + 1, :]
        xc_ref[slab(t), :] = xc

    a, mult, gated = _lru_coeffs(xc_ref[...], wbd_ref, ba_ref[...], bx_ref[...], lam_ref[...])
    b = mult * gated
    h = h0_ref[...]
    for t in range(nt):
        lo, hi = t * nb, (t + 1) * nb
        h = a[lo:hi] * h + b[lo:hi]
        ya_ref[slab(t), :] = (h * gg_ref[slab(t), :]).astype(BF16)
    ht_ref[...] = h

    for t in range(nt):
        s = bsx_ref[t:t + 1, :]
        for sp in range(t + 1):
            s = s + wsx_ref[t * nt + sp:t * nt + sp + 1, :] * vn_ref[slab(sp), :]
        yb_ref[slab(t), :] = (gu_ref[slab(t), :] * s).astype(BF16)


def _sample_mix(proj, sconv_tm, h0, conv_w, conv_b, wbd, ba, bx, lam, wsx, bsx, *, nb, nt):
    rows = nb * nt
    kern = functools.partial(_sample_mix_kernel, nb=nb, nt=nt)
    vec = pl.BlockSpec((1, D_RNN), lambda i: (0, 0))

    def col(jb):
        return pl.BlockSpec((rows, D_RNN), lambda i: (0, jb))

    return pl.pallas_call(
        kern,
        grid=(1,),
        in_specs=[
            col(0), col(1), col(2), col(3),
            pl.BlockSpec((CONV_W - 1, nb, D_RNN), lambda i: (0, 0, 0)),
            pl.BlockSpec((nb, D_RNN), lambda i: (0, 0)),
            pl.BlockSpec((CONV_W, D_RNN), lambda i: (0, 0)),
            vec,
            pl.BlockSpec((D_RNN // V7X_MXU_DIM, V7X_MXU_DIM, 2 * V7X_MXU_DIM), lambda i: (0, 0, 0)),
            vec, vec, vec,
            pl.BlockSpec((nt * nt, D_GMLP), lambda i: (0, 0)),
            pl.BlockSpec((nt, D_GMLP), lambda i: (0, 0)),
        ],
        out_specs=[
            pl.BlockSpec((rows, D_RNN), lambda i: (0, 0)),
            pl.BlockSpec((rows, D_GMLP), lambda i: (0, 0)),
            pl.BlockSpec((nb, D_RNN), lambda i: (0, 0)),
        ],
        out_shape=[
            jax.ShapeDtypeStruct((rows, D_RNN), BF16),
            jax.ShapeDtypeStruct((rows, D_GMLP), BF16),
            jax.ShapeDtypeStruct((nb, D_RNN), F32),
        ],
        scratch_shapes=[pltpu.VMEM((rows, D_RNN), F32)],
        compiler_params=_params("arbitrary"),
        name="sample_mix",
    )(proj, proj, proj, proj, sconv_tm, h0, conv_w, conv_b, wbd, ba, bx, lam, wsx, bsx)


def _merge_kernel(ya_ref, yb_ref, ga_ref, gb_ref, x_ref, mod_ref, wpa_ref, wpb_ref, wo_ref,
                  lng_ref, lnb_ref, o_ref, *, slabs):
    y_a = _dot(ya_ref[...], wpa_ref[...])
    y_b = _dot(yb_ref[...], wpb_ref[...])
    m = (ga_ref[...] * y_a + gb_ref[...] * y_b).astype(BF16)
    mix = _dot(m, wo_ref[...])
    for r0, nr in slabs:
        y = ALPHA * x_ref[r0:r0 + nr, :] + mod_ref[0, 0] * mix[r0:r0 + nr, :]
        o_ref[r0:r0 + nr, :] = _layer_norm(y, lng_ref[...], lnb_ref[...])


def _merge(ya, yb, proj, x, gate, w_pa, w_pb, w_out, ln_g, ln_b, *, tm, rows_per_mod):
    rows = x.shape[0]
    mod_rows = gate.shape[2]
    kern = functools.partial(_merge_kernel, slabs=_slabs(tm, mod_rows))
    tiles_per_mod = rows_per_mod // tm
    return pl.pallas_call(
        kern,
        grid=(rows // tm,),
        in_specs=[
            pl.BlockSpec((tm, D_RNN), lambda i: (i, 0)),
            pl.BlockSpec((tm, D_GMLP), lambda i: (i, 0)),
            pl.BlockSpec((tm, D_MODEL), lambda i: (i, 2)),
            pl.BlockSpec((tm, D_MODEL), lambda i: (i, 3)),
            pl.BlockSpec((tm, D_MODEL), lambda i: (i, 0)),
            pl.BlockSpec((1, 1, mod_rows, D_MODEL), lambda i: (i // tiles_per_mod, 0, 0, 0)),
            pl.BlockSpec((D_RNN, D_MODEL), lambda i: (0, 0)),
            pl.BlockSpec((D_GMLP, D_MODEL), lambda i: (0, 0)),
            pl.BlockSpec((D_MODEL, D_MODEL), lambda i: (0, 0)),
            pl.BlockSpec((1, D_MODEL), lambda i: (0, 0)),
            pl.BlockSpec((1, D_MODEL), lambda i: (0, 0)),
        ],
        out_specs=pl.BlockSpec((tm, D_MODEL), lambda i: (i, 0)),
        out_shape=jax.ShapeDtypeStruct((rows, D_MODEL), F32),
        compiler_params=_params("arbitrary"),
        name="merge_out",
    )(ya, yb, proj, proj, x, gate, w_pa, w_pb, w_out, ln_g, ln_b)


def _block_diag_tiles(w):
    per = V7X_MXU_DIM // RNN_BW
    w4 = w.reshape(RNN_BLOCKS // per, per, RNN_BW, RNN_BW)
    eye = jnp.eye(per, dtype=w.dtype)
    t = w4[:, :, :, None, :] * eye[None, :, None, :, None]
    return t.reshape(RNN_BLOCKS // per, V7X_MXU_DIM, V7X_MXU_DIM)


def kernel(x_prompt, x_sample, state_conv, state_h, c_prompt, c_sample, w_ada, b_ada, ffn1_w_gu, ffn1_w_down, ffn2_w_gu, ffn2_w_down, w_in, conv_w, conv_b, lru_wa, lru_ba, lru_wx, lru_bx, lru_lambda, gmlp_ln_g, gmlp_ln_b, gmlp_ws, gmlp_bs, w_pa, w_pb, w_out, ln_g, ln_b):
    assert w_ada.shape[0] == DEPTH == 1
    bp, seq, _ = x_prompt.shape
    bs, dseq, _ = x_sample.shape
    l = 0

    w_ada_b = w_ada[l].astype(BF16)
    w_gu1, w_dn1 = ffn1_w_gu[l].astype(BF16), ffn1_w_down[l].astype(BF16)
    w_gu2, w_dn2 = ffn2_w_gu[l].astype(BF16), ffn2_w_down[l].astype(BF16)
    w_in_b = w_in[l].astype(BF16)
    w_pa_b, w_pb_b, w_out_b = w_pa[l].astype(BF16), w_pb[l].astype(BF16), w_out[l].astype(BF16)
    wbd = jnp.concatenate([_block_diag_tiles(lru_wa[l]), _block_diag_tiles(lru_wx[l])],
                          axis=-1).astype(BF16)

    row = lambda v: v.reshape(1, -1)
    cb, ba, bx, lam = row(conv_b[l]), row(lru_ba[l]), row(lru_bx[l]), row(lru_lambda[l])
    gln_g, gln_b = row(gmlp_ln_g[l]), row(gmlp_ln_b[l])
    lng = [row(ln_g[l, k]) for k in range(3)]
    lnb = [row(ln_b[l, k]) for k in range(3)]

    bs_tile = jnp.repeat(gmlp_bs[l].T, GMLP_GW, axis=1)
    wsx = jnp.repeat(jnp.transpose(gmlp_ws[l][:, :dseq, :dseq], (1, 2, 0)), GMLP_GW, axis=2)
    wsx = wsx.reshape(dseq * dseq, D_GMLP)
    bsx = bs_tile[:dseq]

    c_all = jnp.concatenate([c_prompt, c_sample], axis=0)
    mod = _modulation(c_all, w_ada_b, row(b_ada[l]))
    mod_p = mod[:bp].reshape(bp, N_MOD, 1, D_MODEL)
    mod_s = jnp.transpose(mod[bp:].reshape(bs, N_MOD, D_MODEL), (1, 0, 2))[None]

    xp = x_prompt.reshape(bp * seq, D_MODEL)
    xs = jnp.transpose(x_sample, (1, 0, 2)).reshape(dseq * bs, D_MODEL)
    n_s = dseq * bs

    def layer(x, modx, rows_per_mod, tm):
        ffn = functools.partial(_ffn, tm=tm, tf=512, rows_per_mod=rows_per_mod)
        x1 = ffn(x, modx[:, 0:3], w_gu1, w_dn1, lng[0], lnb[0])
        proj = _in_proj(x1, modx[:, 3:6], w_in_b, gln_g, gln_b, tm=tm, rows_per_mod=rows_per_mod)
        return x1, proj

    x1p, proj_p = layer(xp, mod_p, seq, 512)
    ya_p, ht_p = _prompt_lru(proj_p, conv_w[l], cb, wbd, ba, bx, lam, batch=bp, seq=seq, tt=256)
    yb_p = _prompt_gmlp(proj_p, gmlp_ws[l], bs_tile, tm=512)
    x2p = _merge(ya_p, yb_p, proj_p, x1p, mod_p[:, 5:6], w_pa_b, w_pb_b, w_out_b, lng[1], lnb[1],
                 tm=256, rows_per_mod=seq)
    x3p = _ffn(x2p, mod_p[:, 6:9], w_gu2, w_dn2, lng[2], lnb[2], tm=512, tf=512, rows_per_mod=seq)

    x1s, proj_s = layer(xs, mod_s, n_s, n_s)
    sconv_tm = jnp.transpose(state_conv[l], (1, 0, 2))
    ya_s, yb_s, ht_s = _sample_mix(proj_s, sconv_tm, state_h[l], conv_w[l], cb, wbd, ba, bx, lam,
                                   wsx, bsx, nb=bs, nt=dseq)
    x2s = _merge(ya_s, yb_s, proj_s, x1s, mod_s[:, 5:6], w_pa_b, w_pb_b, w_out_b, lng[1], lnb[1],
                 tm=256, rows_per_mod=n_s)
    x3s = _ffn(x2s, mod_s[:, 6:9], w_gu2, w_dn2, lng[2], lnb[2], tm=n_s, tf=512, rows_per_mod=n_s)

    y_prompt = x3p.reshape(bp, seq, D_MODEL)
    y_sample = jnp.transpose(x3s.reshape(dseq, bs, D_MODEL), (1, 0, 2))
    xr_p = proj_p[:, :D_RNN].reshape(bp, seq, D_RNN)
    new_conv_p = xr_p[:, seq - (CONV_W - 1):][None]
    new_h_p = ht_p.reshape(bp, D_RNN)[None]
    xr_s = proj_s[:, :D_RNN].reshape(dseq, bs, D_RNN)
    new_conv_s = jnp.transpose(xr_s[dseq - (CONV_W - 1):], (1, 0, 2))[None]
    new_h_s = ht_s[None]
    vn_s = proj_s[:, 2 * D_RNN + D_GMLP:2 * D_RNN + 2 * D_GMLP].reshape(dseq, bs, D_GMLP)
    new_v_s = jnp.transpose(vn_s, (1, 0, 2))[None]
    return (y_prompt, y_sample, new_conv_p, new_h_p, new_conv_s, new_h_s, new_v_s)
```

```python
import functools
import math

import jax
import jax.numpy as jnp
from jax import lax
from jax.experimental import pallas as pl
from jax.experimental.pallas import tpu as pltpu

D_MODEL = 2048
D_RNN = 1024
RNN_BLOCKS = 16
RNN_BW = D_RNN // RNN_BLOCKS
CONV_W = 4
LRU_C = 8.0
D_GMLP = 1024
CHUNK = 128
GMLP_GROUPS = 8
GMLP_GW = D_GMLP // GMLP_GROUPS
D_FF = 5632
N_MOD = 9
IN_COLS = 2 * D_RNN + 2 * D_GMLP + 2 * D_MODEL
LN_EPS = 1e-5
DEPTH = 1
ALPHA = (2.0 * DEPTH) ** 0.25

V7X_SUBLANES = 8
V7X_LANES = 128
V7X_MXU_DIM = 256
V7X_VMEM_LIMIT_BYTES = 58 * 1024 * 1024

BF16 = jnp.bfloat16
F32 = jnp.float32


def _dot(a, b):
    return jnp.dot(a, b, preferred_element_type=F32)


def _layer_norm(x, g, b):
    mu = jnp.mean(x, axis=-1, keepdims=True)
    xc = x - mu
    var = jnp.mean(xc * xc, axis=-1, keepdims=True)
    return xc * lax.rsqrt(var + LN_EPS) * g + b


def _gelu_tanh(x):
    c = math.sqrt(2.0 / math.pi)
    return 0.5 * x * (1.0 + jnp.tanh(c * (x + 0.044715 * (x * x * x))))


def _softplus(x):
    return jnp.maximum(x, 0.0) + jnp.log1p(jnp.exp(-jnp.abs(x)))


def _params(*semantics):
    return pltpu.CompilerParams(dimension_semantics=semantics,
                                vmem_limit_bytes=V7X_VMEM_LIMIT_BYTES)


def _slabs(tm, mod_rows):
    if mod_rows == 1:
        return [(0, tm)]
    return [(s, mod_rows) for s in range(0, tm, mod_rows)]


def _modulated_bf16(x_ref, mod_ref, slabs):
    parts = [(x_ref[r0:r0 + nr, :] * (1.0 + mod_ref[0, 1]) + mod_ref[0, 0]).astype(BF16)
             for r0, nr in slabs]
    return parts[0] if len(parts) == 1 else jnp.concatenate(parts, axis=0)


def _mod_kernel(c_ref, w_ref, b_ref, o_ref):
    c = c_ref[...]
    a = (c * jax.nn.sigmoid(c)).astype(BF16)
    o_ref[...] = _dot(a, w_ref[...].astype(BF16)) + b_ref[...]


def _modulation(c_all, w_ada, b_ada, tn=1024):
    rows = c_all.shape[0]
    n = w_ada.shape[1]
    return pl.pallas_call(
        _mod_kernel,
        grid=(n // tn,),
        in_specs=[
            pl.BlockSpec((rows, D_MODEL), lambda j: (0, 0)),
            pl.BlockSpec((D_MODEL, tn), lambda j: (0, j)),
            pl.BlockSpec((1, tn), lambda j: (0, j)),
        ],
        out_specs=pl.BlockSpec((rows, tn), lambda j: (0, j)),
        out_shape=jax.ShapeDtypeStruct((rows, n), F32),
        compiler_params=_params("arbitrary"),
        name="adaln_mod",
    )(c_all, w_ada, b_ada)


def _ffn_kernel(x_ref, mod_ref, wg_ref, wv_ref, wd_ref, lng_ref, lnb_ref, o_ref, u_ref,
                *, slabs, n_f):
    j = pl.program_id(1)

    @pl.when(j == 0)
    def _():
        u_ref[...] = _modulated_bf16(x_ref, mod_ref, slabs)
        o_ref[...] = jnp.zeros_like(o_ref)

    u = u_ref[...]
    g = _dot(u, wg_ref[...].astype(BF16))
    v = _dot(u, wv_ref[...].astype(BF16))
    h = (g * jax.nn.sigmoid(g) * v).astype(BF16)
    o_ref[...] += _dot(h, wd_ref[...].astype(BF16))

    @pl.when(j == n_f - 1)
    def _():
        for r0, nr in slabs:
            rows = pl.ds(r0, nr)
            y = ALPHA * x_ref[rows, :] + 0.5 * mod_ref[0, 2] * o_ref[rows, :]
            o_ref[rows, :] = _layer_norm(y, lng_ref[...], lnb_ref[...])


def _ffn(x, mod, w_gu, w_down, ln_g, ln_b, *, tm, tf, rows_per_mod):
    rows = x.shape[0]
    mod_rows = mod.shape[2]
    n_f = D_FF // tf
    kern = functools.partial(_ffn_kernel, slabs=_slabs(tm, mod_rows), n_f=n_f)
    tiles_per_mod = rows_per_mod // tm
    return pl.pallas_call(
        kern,
        grid=(rows // tm, n_f),
        in_specs=[
            pl.BlockSpec((tm, D_MODEL), lambda i, j: (i, 0)),
            pl.BlockSpec((1, 3, mod_rows, D_MODEL), lambda i, j: (i // tiles_per_mod, 0, 0, 0)),
            pl.BlockSpec((D_MODEL, tf), lambda i, j: (0, j)),
            pl.BlockSpec((D_MODEL, tf), lambda i, j: (0, n_f + j)),
            pl.BlockSpec((tf, D_MODEL), lambda i, j: (j, 0)),
            pl.BlockSpec((1, D_MODEL), lambda i, j: (0, 0)),
            pl.BlockSpec((1, D_MODEL), lambda i, j: (0, 0)),
        ],
        out_specs=pl.BlockSpec((tm, D_MODEL), lambda i, j: (i, 0)),
        out_shape=jax.ShapeDtypeStruct((rows, D_MODEL), F32),
        scratch_shapes=[pltpu.VMEM((tm, D_MODEL), BF16)],
        compiler_params=_params("arbitrary", "arbitrary"),
        name="macaron_ffn",
    )(x, mod, w_gu, w_gu, w_down, ln_g, ln_b)


_PROJ_TN = 1024


def _proj_kernel(x_ref, mod_ref, w_ref, *rest, slabs, epilogue):
    if epilogue == "layer_norm":
        lng_ref, lnb_ref, o_ref, wb_ref = rest
    else:
        o_ref, wb_ref = rest

    @pl.when(pl.program_id(1) == 0)
    def _():
        wb_ref[...] = w_ref[...].astype(BF16)

    r = _dot(_modulated_bf16(x_ref, mod_ref, slabs), wb_ref[...])
    if epilogue == "gelu":
        r = _gelu_tanh(r)
    elif epilogue == "sigmoid":
        r = jax.nn.sigmoid(r)
    elif epilogue == "layer_norm":
        r = _layer_norm(r, lng_ref[...], lnb_ref[...])
    o_ref[...] = r


def _in_proj(x, mod, w_in, *, col0, col_stride, n_cols, epilogue, ln=None, tm, rows_per_mod):
    rows = x.shape[0]
    mod_rows = mod.shape[2]
    kern = functools.partial(_proj_kernel, slabs=_slabs(tm, mod_rows), epilogue=epilogue)
    tiles_per_mod = rows_per_mod // tm
    in_specs = [
        pl.BlockSpec((tm, D_MODEL), lambda j, i: (i, 0)),
        pl.BlockSpec((1, 3, mod_rows, D_MODEL), lambda j, i: (i // tiles_per_mod, 0, 0, 0)),
        pl.BlockSpec((D_MODEL, _PROJ_TN), lambda j, i: (0, col0 + col_stride * j)),
    ]
    args = [x, mod, w_in]
    if ln is not None:
        in_specs += [pl.BlockSpec((1, _PROJ_TN), lambda j, i: (0, 0))] * 2
        args += list(ln)
    return pl.pallas_call(
        kern,
        grid=(n_cols, rows // tm),
        in_specs=in_specs,
        out_specs=pl.BlockSpec((tm, _PROJ_TN), lambda j, i: (i, j)),
        out_shape=jax.ShapeDtypeStruct((rows, n_cols * _PROJ_TN), F32),
        scratch_shapes=[pltpu.VMEM((D_MODEL, _PROJ_TN), BF16)],
        compiler_params=_params("arbitrary", "arbitrary"),
        name="in_proj_" + epilogue,
    )(*args)


def _lru_coeffs(xc, wbd_ref, ba, bx, lam):
    xcb = xc.astype(BF16)
    ra, rx = [], []
    for q in range(D_RNN // V7X_MXU_DIM):
        cols = slice(q * V7X_MXU_DIM, (q + 1) * V7X_MXU_DIM)
        rq = _dot(xcb[:, cols], wbd_ref[q])
        ra.append(rq[:, :V7X_MXU_DIM])
        rx.append(rq[:, V7X_MXU_DIM:])
    r = jax.nn.sigmoid(jnp.concatenate(ra, axis=-1) + ba)
    i = jax.nn.sigmoid(jnp.concatenate(rx, axis=-1) + bx)
    log_a = (-LRU_C * r) * _softplus(-lam)
    a = jnp.exp(log_a)
    z = -jnp.tanh(log_a) * (a * a + 1.0)
    mult = jnp.where(z > 0.0, z * lax.rsqrt(z), 0.0)
    return a, mult, i * xc


def _prompt_lru_kernel(xr_ref, gg_ref, cw_ref, cb_ref, wbd_ref, ba_ref, bx_ref, lam_ref,
                       ya_ref, ht_ref, tail_ref, h_ref, hs_ref, *, tt):
    t = pl.program_id(1)
    ng = tt // V7X_SUBLANES

    @pl.when(t == 0)
    def _():
        tail_ref[...] = jnp.zeros_like(tail_ref)
        h_ref[...] = jnp.zeros_like(h_ref)

    xr = xr_ref[...]
    prev = tail_ref[...]
    row = lax.broadcasted_iota(jnp.int32, (tt, D_RNN), 0)
    row8 = lax.broadcasted_iota(jnp.int32, (V7X_SUBLANES, D_RNN), 0)

    xc = cb_ref[...]
    for k in range(CONV_W):
        s = CONV_W - 1 - k
        if s == 0:
            xs = xr
        else:
            rolled = pltpu.roll(xr, s, 0)
            head = jnp.where(row8 < s, pltpu.roll(prev, s, 0), rolled[:V7X_SUBLANES])
            xs = jnp.concatenate([head, rolled[V7X_SUBLANES:]], axis=0)
        xc = xc + xs * cw_ref[k:k + 1, :]
    tail_ref[...] = xr[tt - V7X_SUBLANES:, :]

    a, mult, gated = _lru_coeffs(xc, wbd_ref, ba_ref[...], bx_ref[...], lam_ref[...])
    mult = jnp.where(jnp.logical_and(row == 0, t == 0), 1.0, mult)
    b = mult * gated

    a = a.reshape(ng, V7X_SUBLANES, D_RNN)
    b = b.reshape(ng, V7X_SUBLANES, D_RNN)
    sub = lax.broadcasted_iota(jnp.int32, (ng, V7X_SUBLANES, D_RNN), 1)
    s = 1
    while s < V7X_SUBLANES:
        keep = sub >= s
        a_prev = jnp.where(keep, pltpu.roll(a, s, 1), 1.0)
        b_prev = jnp.where(keep, pltpu.roll(b, s, 1), 0.0)
        b = a * b_prev + b
        a = a * a_prev
        s *= 2
    h_in = h_ref[...]
    for g in range(ng):
        h = a[g] * h_in + b[g]
        hs_ref[g * V7X_SUBLANES:(g + 1) * V7X_SUBLANES, :] = h
        h_in = h[V7X_SUBLANES - 1:, :]
    h_ref[...] = h_in
    ht_ref[0] = h_in
    ya_ref[...] = (hs_ref[...] * gg_ref[...]).astype(BF16)


def _prompt_lru(xg, gg, conv_w, conv_b, wbd, ba, bx, lam, *, batch, seq, tt):
    nt = seq // tt
    kern = functools.partial(_prompt_lru_kernel, tt=tt)
    vec = pl.BlockSpec((1, D_RNN), lambda b, t: (0, 0))
    return pl.pallas_call(
        kern,
        grid=(batch, nt),
        in_specs=[
            pl.BlockSpec((tt, D_RNN), lambda b, t: (b * nt + t, 0)),
            pl.BlockSpec((tt, D_RNN), lambda b, t: (b * nt + t, 0)),
            pl.BlockSpec((CONV_W, D_RNN), lambda b, t: (0, 0)),
            vec,
            pl.BlockSpec((D_RNN // V7X_MXU_DIM, V7X_MXU_DIM, 2 * V7X_MXU_DIM), lambda b, t: (0, 0, 0)),
            vec, vec, vec,
        ],
        out_specs=[
            pl.BlockSpec((tt, D_RNN), lambda b, t: (b * nt + t, 0)),
            pl.BlockSpec((1, 1, D_RNN), lambda b, t: (b, 0, 0)),
        ],
        out_shape=[
            jax.ShapeDtypeStruct((batch * seq, D_RNN), BF16),
            jax.ShapeDtypeStruct((batch, 1, D_RNN), F32),
        ],
        scratch_shapes=[pltpu.VMEM((V7X_SUBLANES, D_RNN), F32), pltpu.VMEM((1, D_RNN), F32),
                        pltpu.VMEM((tt, D_RNN), F32)],
        compiler_params=_params("arbitrary", "arbitrary"),
        name="prompt_rglru",
    )(xg, gg, conv_w, conv_b, wbd, ba, bx, lam)


def _prompt_gmlp_kernel(gu_ref, vn_ref, ws_ref, bs_ref, yb_ref, *, tm):
    ti = lax.broadcasted_iota(jnp.int32, (CHUNK, CHUNK), 0)
    si = lax.broadcasted_iota(jnp.int32, (CHUNK, CHUNK), 1)
    causal = si <= ti
    for g in range(GMLP_GROUPS):
        w = jnp.where(causal, ws_ref[g], 0.0).astype(BF16)
        cols = pl.ds(g * GMLP_GW, GMLP_GW)
        bias = bs_ref[:, cols]
        for c in range(tm // CHUNK):
            rows = pl.ds(c * CHUNK, CHUNK)
            s = _dot(w, vn_ref[rows, cols].astype(BF16)) + bias
            yb_ref[rows, cols] = (gu_ref[rows, cols] * s).astype(BF16)


def _prompt_gmlp(xg, vn, ws, bs_tile, *, tm):
    rows = vn.shape[0]
    kern = functools.partial(_prompt_gmlp_kernel, tm=tm)
    return pl.pallas_call(
        kern,
        grid=(rows // tm,),
        in_specs=[
            pl.BlockSpec((tm, D_GMLP), lambda i: (i, 1)),
            pl.BlockSpec((tm, D_GMLP), lambda i: (i, 0)),
            pl.BlockSpec((GMLP_GROUPS, CHUNK, CHUNK), lambda i: (0, 0, 0)),
            pl.BlockSpec((CHUNK, D_GMLP), lambda i: (0, 0)),
        ],
        out_specs=pl.BlockSpec((tm, D_GMLP), lambda i: (i, 0)),
        out_shape=jax.ShapeDtypeStruct((rows, D_GMLP), BF16),
        compiler_params=_params("arbitrary"),
        name="prompt_gmlp",
    )(xg, vn, ws, bs_tile)


def _sample_mix_kernel(xr_ref, gu_ref, gg_ref, vn_ref, sconv_ref, h0_ref, cw_ref, cb_ref,
                       wbd_ref, ba_ref, bx_ref, lam_ref, wsx_ref, bsx_ref,
                       ya_ref, yb_ref, ht_ref, xc_ref, *, nb, nt):
    def slab(t):
        return pl.ds(t * nb, nb)

    xpad = [sconv_ref[k] for k in range(CONV_W - 1)] + [xr_ref[slab(t), :] for t in range(nt)]
    for t in range(nt):
        xc = cb_ref[...]
        for k in range(CONV_W):
            xc = xc + xpad[t + k] * cw_ref[k:k + 1, :]
        xc_ref[slab(t), :] = xc

    a, mult, gated = _lru_coeffs(xc_ref[...], wbd_ref, ba_ref[...], bx_ref[...], lam_ref[...])
    b = mult * gated
    h = h0_ref[...]
    for t in range(nt):
        lo, hi = t * nb, (t + 1) * nb
        h = a[lo:hi] * h + b[lo:hi]
        ya_ref[slab(t), :] = (h * gg_ref[slab(t), :]).astype(BF16)
    ht_ref[...] = h

    for t in range(nt):
        s = bsx_ref[t:t + 1, :]
        for sp in range(t + 1):
            s = s + wsx_ref[t * nt + sp:t * nt + sp + 1, :] * vn_ref[slab(sp), :]
        yb_ref[slab(t), :] = (gu_ref[slab(t), :] * s).astype(BF16)


def _sample_mix(xg, gg, vn, sconv_tm, h0, conv_w, conv_b, wbd, ba, bx, lam, wsx, bsx, *, nb, nt):
    rows = nb * nt
    kern = functools.partial(_sample_mix_kernel, nb=nb, nt=nt)
    vec = pl.BlockSpec((1, D_RNN), lambda i: (0, 0))

    def col(jb):
        return pl.BlockSpec((rows, D_RNN), lambda i: (0, jb))

    return pl.pallas_call(
        kern,
        grid=(1,),
        in_specs=[
            col(0), col(1), col(0), col(0),
            pl.BlockSpec((CONV_W - 1, nb, D_RNN), lambda i: (0, 0, 0)),
            pl.BlockSpec((nb, D_RNN), lambda i: (0, 0)),
            pl.BlockSpec((CONV_W, D_RNN), lambda i: (0, 0)),
            vec,
            pl.BlockSpec((D_RNN // V7X_MXU_DIM, V7X_MXU_DIM, 2 * V7X_MXU_DIM), lambda i: (0, 0, 0)),
            vec, vec, vec,
            pl.BlockSpec((nt * nt, D_GMLP), lambda i: (0, 0)),
            pl.BlockSpec((nt, D_GMLP), lambda i: (0, 0)),
        ],
        out_specs=[
            pl.BlockSpec((rows, D_RNN), lambda i: (0, 0)),
            pl.BlockSpec((rows, D_GMLP), lambda i: (0, 0)),
            pl.BlockSpec((nb, D_RNN), lambda i: (0, 0)),
        ],
        out_shape=[
            jax.ShapeDtypeStruct((rows, D_RNN), BF16),
            jax.ShapeDtypeStruct((rows, D_GMLP), BF16),
            jax.ShapeDtypeStruct((nb, D_RNN), F32),
        ],
        scratch_shapes=[pltpu.VMEM((rows, D_RNN), F32)],
        compiler_params=_params("arbitrary"),
        name="sample_mix",
    )(xg, xg, gg, vn, sconv_tm, h0, conv_w, conv_b, wbd, ba, bx, lam, wsx, bsx)


def _merge_kernel(ya_ref, yb_ref, ga_ref, gb_ref, x_ref, mod_ref, wpa_ref, wpb_ref, wo_ref,
                  lng_ref, lnb_ref, o_ref, *, slabs):
    y_a = _dot(ya_ref[...], wpa_ref[...])
    y_b = _dot(yb_ref[...], wpb_ref[...])
    m = (ga_ref[...] * y_a + gb_ref[...] * y_b).astype(BF16)
    mix = _dot(m, wo_ref[...])
    for r0, nr in slabs:
        y = ALPHA * x_ref[r0:r0 + nr, :] + mod_ref[0, 0] * mix[r0:r0 + nr, :]
        o_ref[r0:r0 + nr, :] = _layer_norm(y, lng_ref[...], lnb_ref[...])


def _merge(ya, yb, sg, x, gate, w_pa, w_pb, w_out, ln_g, ln_b, *, tm, rows_per_mod):
    rows = x.shape[0]
    mod_rows = gate.shape[2]
    kern = functools.partial(_merge_kernel, slabs=_slabs(tm, mod_rows))
    tiles_per_mod = rows_per_mod // tm
    return pl.pallas_call(
        kern,
        grid=(rows // tm,),
        in_specs=[
            pl.BlockSpec((tm, D_RNN), lambda i: (i, 0)),
            pl.BlockSpec((tm, D_GMLP), lambda i: (i, 0)),
            pl.BlockSpec((tm, D_MODEL), lambda i: (i, 0)),
            pl.BlockSpec((tm, D_MODEL), lambda i: (i, 1)),
            pl.BlockSpec((tm, D_MODEL), lambda i: (i, 0)),
            pl.BlockSpec((1, 1, mod_rows, D_MODEL), lambda i: (i // tiles_per_mod, 0, 0, 0)),
            pl.BlockSpec((D_RNN, D_MODEL), lambda i: (0, 0)),
            pl.BlockSpec((D_GMLP, D_MODEL), lambda i: (0, 0)),
            pl.BlockSpec((D_MODEL, D_MODEL), lambda i: (0, 0)),
            pl.BlockSpec((1, D_MODEL), lambda i: (0, 0)),
            pl.BlockSpec((1, D_MODEL), lambda i: (0, 0)),
        ],
        out_specs=pl.BlockSpec((tm, D_MODEL), lambda i: (i, 0)),
        out_shape=jax.ShapeDtypeStruct((rows, D_MODEL), F32),
        compiler_params=_params("arbitrary"),
        name="merge_out",
    )(ya, yb, sg, sg, x, gate, w_pa, w_pb, w_out, ln_g, ln_b)


def _block_diag_tiles(w):
    per = V7X_MXU_DIM // RNN_BW
    w4 = w.reshape(RNN_BLOCKS // per, per, RNN_BW, RNN_BW)
    eye = jnp.eye(per, dtype=w.dtype)
    t = w4[:, :, :, None, :] * eye[None, :, None, :, None]
    return t.reshape(RNN_BLOCKS // per, V7X_MXU_DIM, V7X_MXU_DIM)


def kernel(x_prompt, x_sample, state_conv, state_h, c_prompt, c_sample, w_ada, b_ada, ffn1_w_gu, ffn1_w_down, ffn2_w_gu, ffn2_w_down, w_in, conv_w, conv_b, lru_wa, lru_ba, lru_wx, lru_bx, lru_lambda, gmlp_ln_g, gmlp_ln_b, gmlp_ws, gmlp_bs, w_pa, w_pb, w_out, ln_g, ln_b):
    assert w_ada.shape[0] == DEPTH == 1
    bp, seq, _ = x_prompt.shape
    bs, dseq, _ = x_sample.shape
    l = 0

    w_pa_b, w_pb_b, w_out_b = w_pa[l].astype(BF16), w_pb[l].astype(BF16), w_out[l].astype(BF16)
    wbd = jnp.concatenate([_block_diag_tiles(lru_wa[l]), _block_diag_tiles(lru_wx[l])],
                          axis=-1).astype(BF16)

    row = lambda v: v.reshape(1, -1)
    cb, ba, bx, lam = row(conv_b[l]), row(lru_ba[l]), row(lru_bx[l]), row(lru_lambda[l])
    gln = (row(gmlp_ln_g[l]), row(gmlp_ln_b[l]))
    lng = [row(ln_g[l, k]) for k in range(3)]
    lnb = [row(ln_b[l, k]) for k in range(3)]

    bs_tile = jnp.repeat(gmlp_bs[l].T, GMLP_GW, axis=1)
    wsx = jnp.repeat(jnp.transpose(gmlp_ws[l][:, :dseq, :dseq], (1, 2, 0)), GMLP_GW, axis=2)
    wsx = wsx.reshape(dseq * dseq, D_GMLP)
    bsx = bs_tile[:dseq]

    c_all = jnp.concatenate([c_prompt, c_sample], axis=0)
    mod = _modulation(c_all, w_ada[l], row(b_ada[l]))
    mod_p = mod[:bp].reshape(bp, N_MOD, 1, D_MODEL)
    mod_s = jnp.transpose(mod[bp:].reshape(bs, N_MOD, D_MODEL), (1, 0, 2))[None]

    xp = x_prompt.reshape(bp * seq, D_MODEL)
    xs = jnp.transpose(x_sample, (1, 0, 2)).reshape(dseq * bs, D_MODEL)
    n_s = dseq * bs

    def front(x, modx, rows_per_mod, tm_ffn, tm_proj):
        x1 = _ffn(x, modx[:, 0:3], ffn1_w_gu[l], ffn1_w_down[l], lng[0], lnb[0],
                  tm=tm_ffn, tf=256, rows_per_mod=rows_per_mod)
        proj = functools.partial(_in_proj, x1, modx[:, 3:6], w_in[l], tm=tm_proj,
                                 rows_per_mod=rows_per_mod)
        xg = proj(col0=0, col_stride=2, n_cols=2, epilogue="identity")
        gg = proj(col0=1, col_stride=1, n_cols=1, epilogue="gelu")
        vn = proj(col0=3, col_stride=1, n_cols=1, epilogue="layer_norm", ln=gln)
        sg = proj(col0=4, col_stride=1, n_cols=4, epilogue="sigmoid")
        return x1, xg, gg, vn, sg

    def back(x1, ya, yb, sg, modx, rows_per_mod, tm_ffn):
        x2 = _merge(ya, yb, sg, x1, modx[:, 5:6], w_pa_b, w_pb_b, w_out_b, lng[1], lnb[1],
                    tm=256, rows_per_mod=rows_per_mod)
        return _ffn(x2, modx[:, 6:9], ffn2_w_gu[l], ffn2_w_down[l], lng[2], lnb[2],
                    tm=tm_ffn, tf=256, rows_per_mod=rows_per_mod)

    x1p, xg_p, gg_p, vn_p, sg_p = front(xp, mod_p, seq, 1024, 1024)
    ya_p, ht_p = _prompt_lru(xg_p, gg_p, conv_w[l], cb, wbd, ba, bx, lam, batch=bp, seq=seq, tt=256)
    yb_p = _prompt_gmlp(xg_p, vn_p, gmlp_ws[l], bs_tile, tm=512)
    x3p = back(x1p, ya_p, yb_p, sg_p, mod_p, seq, 1024)

    x1s, xg_s, gg_s, vn_s, sg_s = front(xs, mod_s, n_s, n_s, n_s)
    sconv_tm = jnp.transpose(state_conv[l], (1, 0, 2))
    ya_s, yb_s, ht_s = _sample_mix(xg_s, gg_s, vn_s, sconv_tm, state_h[l], conv_w[l], cb, wbd,
                                   ba, bx, lam, wsx, bsx, nb=bs, nt=dseq)
    x3s = back(x1s, ya_s, yb_s, sg_s, mod_s, n_s, n_s)

    keep = CONV_W - 1
    y_prompt = x3p.reshape(bp, seq, D_MODEL)
    y_sample = jnp.transpose(x3s.reshape(dseq, bs, D_MODEL), (1, 0, 2))
    new_conv_p = lax.slice(xg_p.reshape(bp, seq, 2 * D_RNN), (0, seq - keep, 0), (bp, seq, D_RNN))[None]
    new_h_p = ht_p.reshape(bp, D_RNN)[None]
    xr_s = lax.slice(xg_s.reshape(dseq, bs, 2 * D_RNN), (dseq - keep, 0, 0), (dseq, bs, D_RNN))
    new_conv_s = jnp.transpose(xr_s, (1, 0, 2))[None]
    new_h_s = ht_s[None]
    new_v_s = jnp.transpose(vn_s.reshape(dseq, bs, D_GMLP), (1, 0, 2))[None]
    return (y_prompt, y_sample, new_conv_p, new_h_p, new_conv_s, new_h_s, new_v_s)
```

```python
import functools
import math

import jax
import jax.numpy as jnp
from jax import lax
from jax.experimental import pallas as pl
from jax.experimental.pallas import tpu as pltpu

D_MODEL = 2048
D_RNN = 1024
RNN_BLOCKS = 16
RNN_BW = D_RNN // RNN_BLOCKS
CONV_W = 4
LRU_C = 8.0
D_GMLP = 1024
CHUNK = 128
GMLP_GROUPS = 8
GMLP_GW = D_GMLP // GMLP_GROUPS
D_FF = 5632
N_MOD = 9
IN_COLS = 2 * D_RNN + 2 * D_GMLP + 2 * D_MODEL
BRANCH_COLS = 2 * D_RNN + 2 * D_GMLP
LN_EPS = 1e-5
DEPTH = 1
ALPHA = (2.0 * DEPTH) ** 0.25

V7X_SUBLANES = 8
V7X_LANES = 128
V7X_MXU_DIM = 256
V7X_VMEM_LIMIT_BYTES = 58 * 1024 * 1024

BF16 = jnp.bfloat16
F32 = jnp.float32


def _dot(a, b):
    return jnp.dot(a, b, preferred_element_type=F32)


def _layer_norm(x, g, b):
    mu = jnp.mean(x, axis=-1, keepdims=True)
    xc = x - mu
    var = jnp.mean(xc * xc, axis=-1, keepdims=True)
    return xc * lax.rsqrt(var + LN_EPS) * g + b


def _gelu_tanh(x):
    c = math.sqrt(2.0 / math.pi)
    return 0.5 * x * (1.0 + jnp.tanh(c * (x + 0.044715 * (x * x * x))))


def _softplus(x):
    return jnp.maximum(x, 0.0) + jnp.log1p(jnp.exp(-jnp.abs(x)))


def _params(*semantics):
    return pltpu.CompilerParams(dimension_semantics=semantics,
                                vmem_limit_bytes=V7X_VMEM_LIMIT_BYTES)


def _slabs(tm, mod_rows):
    if mod_rows == 1:
        return [(0, tm)]
    return [(s, mod_rows) for s in range(0, tm, mod_rows)]


def _modulated_bf16(x_ref, mod_ref, slabs):
    parts = [(x_ref[r0:r0 + nr, :] * (1.0 + mod_ref[0, 1]) + mod_ref[0, 0]).astype(BF16)
             for r0, nr in slabs]
    return parts[0] if len(parts) == 1 else jnp.concatenate(parts, axis=0)


def _mod_kernel(c_ref, w_ref, b_ref, o_ref):
    c = c_ref[...]
    a = (c * jax.nn.sigmoid(c)).astype(BF16)
    o_ref[...] = _dot(a, w_ref[...].astype(BF16)) + b_ref[...]


def _modulation(c_all, w_ada, b_ada, tn=1024):
    rows = c_all.shape[0]
    n = w_ada.shape[1]
    return pl.pallas_call(
        _mod_kernel,
        grid=(n // tn,),
        in_specs=[
            pl.BlockSpec((rows, D_MODEL), lambda j: (0, 0)),
            pl.BlockSpec((D_MODEL, tn), lambda j: (0, j)),
            pl.BlockSpec((1, tn), lambda j: (0, j)),
        ],
        out_specs=pl.BlockSpec((rows, tn), lambda j: (0, j)),
        out_shape=jax.ShapeDtypeStruct((rows, n), F32),
        compiler_params=_params("arbitrary"),
        name="adaln_mod",
    )(c_all, w_ada, b_ada)


def _ffn_kernel(x_ref, mod_ref, wg_ref, wv_ref, wd_ref, lng_ref, lnb_ref, o_ref, u_ref,
                *, slabs, n_f):
    j = pl.program_id(1)

    @pl.when(j == 0)
    def _():
        u_ref[...] = _modulated_bf16(x_ref, mod_ref, slabs)
        o_ref[...] = jnp.zeros_like(o_ref)

    u = u_ref[...]
    g = _dot(u, wg_ref[...].astype(BF16))
    v = _dot(u, wv_ref[...].astype(BF16))
    h = (g * jax.nn.sigmoid(g) * v).astype(BF16)
    o_ref[...] += _dot(h, wd_ref[...].astype(BF16))

    @pl.when(j == n_f - 1)
    def _():
        for r0, nr in slabs:
            rows = pl.ds(r0, nr)
            y = ALPHA * x_ref[rows, :] + 0.5 * mod_ref[0, 2] * o_ref[rows, :]
            o_ref[rows, :] = _layer_norm(y, lng_ref[...], lnb_ref[...])


def _ffn(x, mod, w_gu, w_down, ln_g, ln_b, *, tm, tf, rows_per_mod):
    rows = x.shape[0]
    mod_rows = mod.shape[2]
    n_f = D_FF // tf
    kern = functools.partial(_ffn_kernel, slabs=_slabs(tm, mod_rows), n_f=n_f)
    tiles_per_mod = rows_per_mod // tm
    return pl.pallas_call(
        kern,
        grid=(rows // tm, n_f),
        in_specs=[
            pl.BlockSpec((tm, D_MODEL), lambda i, j: (i, 0)),
            pl.BlockSpec((1, 3, mod_rows, D_MODEL), lambda i, j: (i // tiles_per_mod, 0, 0, 0)),
            pl.BlockSpec((D_MODEL, tf), lambda i, j: (0, j)),
            pl.BlockSpec((D_MODEL, tf), lambda i, j: (0, n_f + j)),
            pl.BlockSpec((tf, D_MODEL), lambda i, j: (j, 0)),
            pl.BlockSpec((1, D_MODEL), lambda i, j: (0, 0)),
            pl.BlockSpec((1, D_MODEL), lambda i, j: (0, 0)),
        ],
        out_specs=pl.BlockSpec((tm, D_MODEL), lambda i, j: (i, 0)),
        out_shape=jax.ShapeDtypeStruct((rows, D_MODEL), F32),
        scratch_shapes=[pltpu.VMEM((tm, D_MODEL), BF16)],
        compiler_params=_params("arbitrary", "arbitrary"),
        name="macaron_ffn",
    )(x, mod, w_gu, w_gu, w_down, ln_g, ln_b)


def _lru_coeffs(xc, wbd_ref, ba, bx, lam):
    xcb = xc.astype(BF16)
    ra, rx = [], []
    for q in range(D_RNN // V7X_MXU_DIM):
        cols = slice(q * V7X_MXU_DIM, (q + 1) * V7X_MXU_DIM)
        rq = _dot(xcb[:, cols], wbd_ref[q])
        ra.append(rq[:, :V7X_MXU_DIM])
        rx.append(rq[:, V7X_MXU_DIM:])
    r = jax.nn.sigmoid(jnp.concatenate(ra, axis=-1) + ba)
    i = jax.nn.sigmoid(jnp.concatenate(rx, axis=-1) + bx)
    log_a = (-LRU_C * r) * _softplus(-lam)
    a = jnp.exp(log_a)
    z = -jnp.tanh(log_a) * (a * a + 1.0)
    mult = jnp.where(z > 0.0, z * lax.rsqrt(z), 0.0)
    return a, mult, i * xc


def _prompt_mixer_kernel(x_ref, mod_ref, w_ref, cw_ref, cb_ref, wbd_ref, ba_ref, bx_ref, lam_ref,
                         glng_ref, glnb_ref, ws_ref, bs_ref,
                         ya_ref, yb_ref, ht_ref, tailo_ref,
                         tail_ref, h_ref, hs_ref, *, tt):
    t = pl.program_id(1)
    ng = tt // V7X_SUBLANES

    @pl.when(t == 0)
    def _():
        tail_ref[...] = jnp.zeros_like(tail_ref)
        h_ref[...] = jnp.zeros_like(h_ref)

    u = _modulated_bf16(x_ref, mod_ref, [(0, tt)])

    xr = _dot(u, w_ref[:, 0:D_RNN])
    prev = tail_ref[...]
    row = lax.broadcasted_iota(jnp.int32, (tt, D_RNN), 0)
    row8 = lax.broadcasted_iota(jnp.int32, (V7X_SUBLANES, D_RNN), 0)
    xc = cb_ref[...]
    for k in range(CONV_W):
        s = CONV_W - 1 - k
        if s == 0:
            xs = xr
        else:
            rolled = pltpu.roll(xr, s, 0)
            head = jnp.where(row8 < s, pltpu.roll(prev, s, 0), rolled[:V7X_SUBLANES])
            xs = jnp.concatenate([head, rolled[V7X_SUBLANES:]], axis=0)
        xc = xc + xs * cw_ref[k:k + 1, :]
    new_tail = xr[tt - V7X_SUBLANES:, :]
    tail_ref[...] = new_tail
    tailo_ref[0] = new_tail

    a, mult, gated = _lru_coeffs(xc, wbd_ref, ba_ref[...], bx_ref[...], lam_ref[...])
    mult = jnp.where(jnp.logical_and(row == 0, t == 0), 1.0, mult)
    b = mult * gated

    a = a.reshape(ng, V7X_SUBLANES, D_RNN)
    b = b.reshape(ng, V7X_SUBLANES, D_RNN)
    sub = lax.broadcasted_iota(jnp.int32, (ng, V7X_SUBLANES, D_RNN), 1)
    s = 1
    while s < V7X_SUBLANES:
        keep = sub >= s
        a_prev = jnp.where(keep, pltpu.roll(a, s, 1), 1.0)
        b_prev = jnp.where(keep, pltpu.roll(b, s, 1), 0.0)
        b = a * b_prev + b
        a = a * a_prev
        s *= 2
    h_in = h_ref[...]
    for g in range(ng):
        h = a[g] * h_in + b[g]
        hs_ref[g * V7X_SUBLANES:(g + 1) * V7X_SUBLANES, :] = h
        h_in = h[V7X_SUBLANES - 1:, :]
    h_ref[...] = h_in
    ht_ref[0] = h_in
    gr = _dot(u, w_ref[:, D_RNN:2 * D_RNN])
    ya_ref[...] = (hs_ref[...] * _gelu_tanh(gr)).astype(BF16)

    gu = _dot(u, w_ref[:, 2 * D_RNN:2 * D_RNN + D_GMLP])
    gv = _dot(u, w_ref[:, 2 * D_RNN + D_GMLP:BRANCH_COLS])
    vn = _layer_norm(gv, glng_ref[...], glnb_ref[...]).astype(BF16)
    ti = lax.broadcasted_iota(jnp.int32, (CHUNK, CHUNK), 0)
    si = lax.broadcasted_iota(jnp.int32, (CHUNK, CHUNK), 1)
    causal = si <= ti
    for g in range(GMLP_GROUPS):
        w = jnp.where(causal, ws_ref[g], 0.0).astype(BF16)
        c0 = g * GMLP_GW
        bias = bs_ref[:, c0:c0 + GMLP_GW]
        for c in range(tt // CHUNK):
            r0 = c * CHUNK
            s_blk = _dot(w, vn[r0:r0 + CHUNK, c0:c0 + GMLP_GW]) + bias
            yb_ref[r0:r0 + CHUNK, c0:c0 + GMLP_GW] = (
                gu[r0:r0 + CHUNK, c0:c0 + GMLP_GW] * s_blk).astype(BF16)


def _prompt_mixer(x, mod, w_branch, conv_w, conv_b, wbd, ba, bx, lam, gln_g, gln_b, ws, bs_tile,
                  *, batch, seq, tt):
    nt = seq // tt
    kern = functools.partial(_prompt_mixer_kernel, tt=tt)
    vec = pl.BlockSpec((1, D_RNN), lambda b, t: (0, 0))
    return pl.pallas_call(
        kern,
        grid=(batch, nt),
        in_specs=[
            pl.BlockSpec((tt, D_MODEL), lambda b, t: (b * nt + t, 0)),
            pl.BlockSpec((1, 3, 1, D_MODEL), lambda b, t: (b, 0, 0, 0)),
            pl.BlockSpec((D_MODEL, BRANCH_COLS), lambda b, t: (0, 0)),
            pl.BlockSpec((CONV_W, D_RNN), lambda b, t: (0, 0)),
            vec,
            pl.BlockSpec((D_RNN // V7X_MXU_DIM, V7X_MXU_DIM, 2 * V7X_MXU_DIM), lambda b, t: (0, 0, 0)),
            vec, vec, vec, vec, vec,
            pl.BlockSpec((GMLP_GROUPS, CHUNK, CHUNK), lambda b, t: (0, 0, 0)),
            pl.BlockSpec((CHUNK, D_GMLP), lambda b, t: (0, 0)),
        ],
        out_specs=[
            pl.BlockSpec((tt, D_RNN), lambda b, t: (b * nt + t, 0)),
            pl.BlockSpec((tt, D_GMLP), lambda b, t: (b * nt + t, 0)),
            pl.BlockSpec((1, 1, D_RNN), lambda b, t: (b, 0, 0)),
            pl.BlockSpec((1, V7X_SUBLANES, D_RNN), lambda b, t: (b, 0, 0)),
        ],
        out_shape=[
            jax.ShapeDtypeStruct((batch * seq, D_RNN), BF16),
            jax.ShapeDtypeStruct((batch * seq, D_GMLP), BF16),
            jax.ShapeDtypeStruct((batch, 1, D_RNN), F32),
            jax.ShapeDtypeStruct((batch, V7X_SUBLANES, D_RNN), F32),
        ],
        scratch_shapes=[pltpu.VMEM((V7X_SUBLANES, D_RNN), F32), pltpu.VMEM((1, D_RNN), F32),
                        pltpu.VMEM((tt, D_RNN), F32)],
        compiler_params=_params("arbitrary", "arbitrary"),
        name="prompt_mixer",
    )(x, mod, w_branch, conv_w, conv_b, wbd, ba, bx, lam, gln_g, gln_b, ws, bs_tile)


_PROJ_TN = 1024


def _sample_proj_kernel(x_ref, mod_ref, w_ref, lng_ref, lnb_ref, o_ref, u_ref, *, slabs):
    j = pl.program_id(0)

    @pl.when(j == 0)
    def _():
        u_ref[...] = _modulated_bf16(x_ref, mod_ref, slabs)

    r = _dot(u_ref[...], w_ref[...])

    @pl.when(jnp.logical_or(j == 0, j == 2))
    def _():
        o_ref[...] = r

    @pl.when(j == 1)
    def _():
        o_ref[...] = _gelu_tanh(r)

    @pl.when(j == 3)
    def _():
        o_ref[...] = _layer_norm(r, lng_ref[...], lnb_ref[...])


def _sample_proj(x, mod, w_branch, gln_g, gln_b):
    rows = x.shape[0]
    mod_rows = mod.shape[2]
    kern = functools.partial(_sample_proj_kernel, slabs=_slabs(rows, mod_rows))
    return pl.pallas_call(
        kern,
        grid=(BRANCH_COLS // _PROJ_TN,),
        in_specs=[
            pl.BlockSpec((rows, D_MODEL), lambda j: (0, 0)),
            pl.BlockSpec((1, 3, mod_rows, D_MODEL), lambda j: (0, 0, 0, 0)),
            pl.BlockSpec((D_MODEL, _PROJ_TN), lambda j: (0, j)),
            pl.BlockSpec((1, D_GMLP), lambda j: (0, 0)),
            pl.BlockSpec((1, D_GMLP), lambda j: (0, 0)),
        ],
        out_specs=pl.BlockSpec((rows, _PROJ_TN), lambda j: (0, j)),
        out_shape=jax.ShapeDtypeStruct((rows, BRANCH_COLS), F32),
        scratch_shapes=[pltpu.VMEM((rows, D_MODEL), BF16)],
        compiler_params=_params("arbitrary"),
        name="sample_proj",
    )(x, mod, w_branch, gln_g, gln_b)


def _sample_mix_kernel(xr_ref, gg_ref, gu_ref, vn_ref, sconv_ref, h0_ref, cw_ref, cb_ref,
                       wbd_ref, ba_ref, bx_ref, lam_ref, wsx_ref, bsx_ref,
                       ya_ref, yb_ref, ht_ref, xc_ref, *, nb, nt):
    def slab(t):
        return pl.ds(t * nb, nb)

    xpad = [sconv_ref[k] for k in range(CONV_W - 1)] + [xr_ref[slab(t), :] for t in range(nt)]
    for t in range(nt):
        xc = cb_ref[...]
        for k in range(CONV_W):
            xc = xc + xpad[t + k] * cw_ref[k:k + 1, :]
        xc_ref[slab(t), :] = xc

    a, mult, gated = _lru_coeffs(xc_ref[...], wbd_ref, ba_ref[...], bx_ref[...], lam_ref[...])
    b = mult * gated
    h = h0_ref[...]
    for t in range(nt):
        lo, hi = t * nb, (t + 1) * nb
        h = a[lo:hi] * h + b[lo:hi]
        ya_ref[slab(t), :] = (h * gg_ref[slab(t), :]).astype(BF16)
    ht_ref[...] = h

    for t in range(nt):
        s = bsx_ref[t:t + 1, :]
        for sp in range(t + 1):
            s = s + wsx_ref[t * nt + sp:t * nt + sp + 1, :] * vn_ref[slab(sp), :]
        yb_ref[slab(t), :] = (gu_ref[slab(t), :] * s).astype(BF16)


def _sample_mix(proj, sconv_tm, h0, conv_w, conv_b, wbd, ba, bx, lam, wsx, bsx, *, nb, nt):
    rows = nb * nt
    kern = functools.partial(_sample_mix_kernel, nb=nb, nt=nt)
    vec = pl.BlockSpec((1, D_RNN), lambda i: (0, 0))

    def col(jb):
        return pl.BlockSpec((rows, D_RNN), lambda i: (0, jb))

    return pl.pallas_call(
        kern,
        grid=(1,),
        in_specs=[
            col(0), col(1), col(2), col(3),
            pl.BlockSpec((CONV_W - 1, nb, D_RNN), lambda i: (0, 0, 0)),
            pl.BlockSpec((nb, D_RNN), lambda i: (0, 0)),
            pl.BlockSpec((CONV_W, D_RNN), lambda i: (0, 0)),
            vec,
            pl.BlockSpec((D_RNN // V7X_MXU_DIM, V7X_MXU_DIM, 2 * V7X_MXU_DIM), lambda i: (0, 0, 0)),
            vec, vec, vec,
            pl.BlockSpec((nt * nt, D_GMLP), lambda i: (0, 0)),
            pl.BlockSpec((nt, D_GMLP), lambda i: (0, 0)),
        ],
        out_specs=[
            pl.BlockSpec((rows, D_RNN), lambda i: (0, 0)),
            pl.BlockSpec((rows, D_GMLP), lambda i: (0, 0)),
            pl.BlockSpec((nb, D_RNN), lambda i: (0, 0)),
        ],
        out_shape=[
            jax.ShapeDtypeStruct((rows, D_RNN), BF16),
            jax.ShapeDtypeStruct((rows, D_GMLP), BF16),
            jax.ShapeDtypeStruct((nb, D_RNN), F32),
        ],
        scratch_shapes=[pltpu.VMEM((rows, D_RNN), F32)],
        compiler_params=_params("arbitrary"),
        name="sample_mix",
    )(proj, proj, proj, proj, sconv_tm, h0, conv_w, conv_b, wbd, ba, bx, lam, wsx, bsx)


def _merge_kernel(ya_ref, yb_ref, x_ref, mod_ref, wgate_ref, wpa_ref, wpb_ref, wo_ref,
                  lng_ref, lnb_ref, o_ref, *, slabs):
    u = _modulated_bf16(x_ref, mod_ref, slabs)
    y_a = jax.nn.sigmoid(_dot(u, wgate_ref[:, :D_MODEL])) * _dot(ya_ref[...], wpa_ref[...])
    y_b = jax.nn.sigmoid(_dot(u, wgate_ref[:, D_MODEL:])) * _dot(yb_ref[...], wpb_ref[...])
    mix = _dot((y_a + y_b).astype(BF16), wo_ref[...])
    for r0, nr in slabs:
        y = ALPHA * x_ref[r0:r0 + nr, :] + mod_ref[0, 2] * mix[r0:r0 + nr, :]
        o_ref[r0:r0 + nr, :] = _layer_norm(y, lng_ref[...], lnb_ref[...])


def _merge(ya, yb, x, mod, w_gate, w_pa, w_pb, w_out, ln_g, ln_b, *, tm, rows_per_mod):
    rows = x.shape[0]
    mod_rows = mod.shape[2]
    kern = functools.partial(_merge_kernel, slabs=_slabs(tm, mod_rows))
    tiles_per_mod = rows_per_mod // tm
    whole = lambda shape: pl.BlockSpec(shape, lambda i: (0, 0))
    return pl.pallas_call(
        kern,
        grid=(rows // tm,),
        in_specs=[
            pl.BlockSpec((tm, D_RNN), lambda i: (i, 0)),
            pl.BlockSpec((tm, D_GMLP), lambda i: (i, 0)),
            pl.BlockSpec((tm, D_MODEL), lambda i: (i, 0)),
            pl.BlockSpec((1, 3, mod_rows, D_MODEL), lambda i: (i // tiles_per_mod, 0, 0, 0)),
            whole((D_MODEL, 2 * D_MODEL)),
            whole((D_RNN, D_MODEL)),
            whole((D_GMLP, D_MODEL)),
            whole((D_MODEL, D_MODEL)),
            whole((1, D_MODEL)),
            whole((1, D_MODEL)),
        ],
        out_specs=pl.BlockSpec((tm, D_MODEL), lambda i: (i, 0)),
        out_shape=jax.ShapeDtypeStruct((rows, D_MODEL), F32),
        compiler_params=_params("arbitrary"),
        name="merge_out",
    )(ya, yb, x, mod, w_gate, w_pa, w_pb, w_out, ln_g, ln_b)


def _block_diag_tiles(w):
    per = V7X_MXU_DIM // RNN_BW
    w4 = w.reshape(RNN_BLOCKS // per, per, RNN_BW, RNN_BW)
    eye = jnp.eye(per, dtype=w.dtype)
    t = w4[:, :, :, None, :] * eye[None, :, None, :, None]
    return t.reshape(RNN_BLOCKS // per, V7X_MXU_DIM, V7X_MXU_DIM)


def kernel(x_prompt, x_sample, state_conv, state_h, c_prompt, c_sample, w_ada, b_ada, ffn1_w_gu, ffn1_w_down, ffn2_w_gu, ffn2_w_down, w_in, conv_w, conv_b, lru_wa, lru_ba, lru_wx, lru_bx, lru_lambda, gmlp_ln_g, gmlp_ln_b, gmlp_ws, gmlp_bs, w_pa, w_pb, w_out, ln_g, ln_b):
    assert w_ada.shape[0] == DEPTH == 1
    bp, seq, _ = x_prompt.shape
    bs, dseq, _ = x_sample.shape
    l = 0

    w_branch = w_in[l, :, :BRANCH_COLS].astype(BF16)
    w_gate = w_in[l, :, BRANCH_COLS:].astype(BF16)
    w_pa_b, w_pb_b, w_out_b = w_pa[l].astype(BF16), w_pb[l].astype(BF16), w_out[l].astype(BF16)
    wbd = jnp.concatenate([_block_diag_tiles(lru_wa[l]), _block_diag_tiles(lru_wx[l])],
                          axis=-1).astype(BF16)

    row = lambda v: v.reshape(1, -1)
    cb, ba, bx, lam = row(conv_b[l]), row(lru_ba[l]), row(lru_bx[l]), row(lru_lambda[l])
    gln_g, gln_b = row(gmlp_ln_g[l]), row(gmlp_ln_b[l])
    lng = [row(ln_g[l, k]) for k in range(3)]
    lnb = [row(ln_b[l, k]) for k in range(3)]

    bs_tile = jnp.repeat(gmlp_bs[l].T, GMLP_GW, axis=1)
    wsx = jnp.repeat(jnp.transpose(gmlp_ws[l][:, :dseq, :dseq], (1, 2, 0)), GMLP_GW, axis=2)
    wsx = wsx.reshape(dseq * dseq, D_GMLP)
    bsx = bs_tile[:dseq]

    c_all = jnp.concatenate([c_prompt, c_sample], axis=0)
    mod = _modulation(c_all, w_ada[l], row(b_ada[l]))
    mod_p = mod[:bp].reshape(bp, N_MOD, 1, D_MODEL)
    mod_s = jnp.transpose(mod[bp:].reshape(bs, N_MOD, D_MODEL), (1, 0, 2))[None]

    xp = x_prompt.reshape(bp * seq, D_MODEL)
    xs = jnp.transpose(x_sample, (1, 0, 2)).reshape(dseq * bs, D_MODEL)
    n_s = dseq * bs
    keep = CONV_W - 1

    def ffn(x, modx, k, w_gu, w_down, tm, tf, rows_per_mod):
        return _ffn(x, modx[:, 3 * k:3 * k + 3], w_gu[l], w_down[l], lng[k], lnb[k],
                    tm=tm, tf=tf, rows_per_mod=rows_per_mod)

    def merge(ya, yb, x1, modx, rows_per_mod):
        return _merge(ya, yb, x1, modx[:, 3:6], w_gate, w_pa_b, w_pb_b, w_out_b, lng[1], lnb[1],
                      tm=256, rows_per_mod=rows_per_mod)

    x1p = ffn(xp, mod_p, 0, ffn1_w_gu, ffn1_w_down, 1024, 256, seq)
    ya_p, yb_p, ht_p, tail_p = _prompt_mixer(
        x1p, mod_p[:, 3:6], w_branch, conv_w[l], cb, wbd, ba, bx, lam, gln_g, gln_b,
        gmlp_ws[l], bs_tile, batch=bp, seq=seq, tt=512)
    x2p = merge(ya_p, yb_p, x1p, mod_p, seq)
    x3p = ffn(x2p, mod_p, 2, ffn2_w_gu, ffn2_w_down, 1024, 256, seq)

    x1s = ffn(xs, mod_s, 0, ffn1_w_gu, ffn1_w_down, n_s, 512, n_s)
    proj_s = _sample_proj(x1s, mod_s[:, 3:6], w_branch, gln_g, gln_b)
    sconv_tm = jnp.transpose(state_conv[l], (1, 0, 2))
    ya_s, yb_s, ht_s = _sample_mix(proj_s, sconv_tm, state_h[l], conv_w[l], cb, wbd, ba, bx, lam,
                                   wsx, bsx, nb=bs, nt=dseq)
    x2s = merge(ya_s, yb_s, x1s, mod_s, n_s)
    x3s = ffn(x2s, mod_s, 2, ffn2_w_gu, ffn2_w_down, n_s, 512, n_s)

    y_prompt = x3p.reshape(bp, seq, D_MODEL)
    y_sample = jnp.transpose(x3s.reshape(dseq, bs, D_MODEL), (1, 0, 2))
    new_conv_p = tail_p[:, V7X_SUBLANES - keep:][None]
    new_h_p = ht_p.reshape(bp, D_RNN)[None]
    xr_s = lax.slice(proj_s.reshape(dseq, bs, BRANCH_COLS), (dseq - keep, 0, 0), (dseq, bs, D_RNN))
    new_conv_s = jnp.transpose(xr_s, (1, 0, 2))[None]
    new_h_s = ht_s[None]
    vn_s = lax.slice(proj_s.reshape(dseq, bs, BRANCH_COLS), (0, 0, 2 * D_RNN + D_GMLP),
                     (dseq, bs, BRANCH_COLS))
    new_v_s = jnp.transpose(vn_s, (1, 0, 2))[None]
    return (y_prompt, y_sample, new_conv_p, new_h_p, new_conv_s, new_h_s, new_v_s)
```

```python
import functools
import math

import jax
import jax.numpy as jnp
from jax import lax
from jax.experimental import pallas as pl
from jax.experimental.pallas import tpu as pltpu

D_MODEL = 2048
D_RNN = 1024
RNN_BLOCKS = 16
RNN_BW = D_RNN // RNN_BLOCKS
CONV_W = 4
LRU_C = 8.0
D_GMLP = 1024
CHUNK = 128
GMLP_GROUPS = 8
GMLP_GW = D_GMLP // GMLP_GROUPS
D_FF = 5632
N_MOD = 9
IN_COLS = 2 * D_RNN + 2 * D_GMLP + 2 * D_MODEL
BRANCH_COLS = 2 * D_RNN + 2 * D_GMLP
LN_EPS = 1e-5
DEPTH = 1
ALPHA = (2.0 * DEPTH) ** 0.25

V7X_SUBLANES = 8
V7X_LANES = 128
V7X_MXU_DIM = 256
V7X_VMEM_LIMIT_BYTES = 58 * 1024 * 1024

BF16 = jnp.bfloat16
F32 = jnp.float32


def _dot(a, b):
    return jnp.dot(a, b, preferred_element_type=F32)


def _layer_norm(x, g, b):
    mu = jnp.mean(x, axis=-1, keepdims=True)
    xc = x - mu
    var = jnp.mean(xc * xc, axis=-1, keepdims=True)
    return xc * lax.rsqrt(var + LN_EPS) * g + b


def _gelu_tanh(x):
    c = math.sqrt(2.0 / math.pi)
    return 0.5 * x * (1.0 + jnp.tanh(c * (x + 0.044715 * (x * x * x))))


def _softplus(x):
    return jnp.maximum(x, 0.0) + jnp.log1p(jnp.exp(-jnp.abs(x)))


def _params(*semantics):
    return pltpu.CompilerParams(dimension_semantics=semantics,
                                vmem_limit_bytes=V7X_VMEM_LIMIT_BYTES)


def _slabs(tm, mod_rows):
    if mod_rows == 1:
        return [(0, tm)]
    return [(s, mod_rows) for s in range(0, tm, mod_rows)]


def _modulated_bf16(x_ref, mod_ref, slabs):
    parts = [(x_ref[r0:r0 + nr, :] * (1.0 + mod_ref[0, 1]) + mod_ref[0, 0]).astype(BF16)
             for r0, nr in slabs]
    return parts[0] if len(parts) == 1 else jnp.concatenate(parts, axis=0)


def _mod_kernel(c_ref, w_ref, b_ref, o_ref):
    c = c_ref[...]
    a = (c * jax.nn.sigmoid(c)).astype(BF16)
    o_ref[...] = _dot(a, w_ref[...].astype(BF16)) + b_ref[...]


def _modulation(c_all, w_ada, b_ada, tn=1024):
    rows = c_all.shape[0]
    n = w_ada.shape[1]
    return pl.pallas_call(
        _mod_kernel,
        grid=(n // tn,),
        in_specs=[
            pl.BlockSpec((rows, D_MODEL), lambda j: (0, 0)),
            pl.BlockSpec((D_MODEL, tn), lambda j: (0, j)),
            pl.BlockSpec((1, tn), lambda j: (0, j)),
        ],
        out_specs=pl.BlockSpec((rows, tn), lambda j: (0, j)),
        out_shape=jax.ShapeDtypeStruct((rows, n), F32),
        compiler_params=_params("arbitrary"),
        name="adaln_mod",
    )(c_all, w_ada, b_ada)


def _ffn_kernel(x_ref, mod_ref, wg_ref, wv_ref, wd_ref, lng_ref, lnb_ref, o_ref, *rest,
                slabs, n_f, emit_bf16):
    if emit_bf16:
        wgb_ref, wvb_ref, wdb_ref, u_ref = rest
        wgb_ref[...] = wg_ref[...].astype(BF16)
        wvb_ref[...] = wv_ref[...].astype(BF16)
        wdb_ref[...] = wd_ref[...].astype(BF16)
        wg_ref, wv_ref, wd_ref = wgb_ref, wvb_ref, wdb_ref
    else:
        (u_ref,) = rest
    j = pl.program_id(1)

    @pl.when(j == 0)
    def _():
        u_ref[...] = _modulated_bf16(x_ref, mod_ref, slabs)
        o_ref[...] = jnp.zeros_like(o_ref)

    u = u_ref[...]
    g = _dot(u, wg_ref[...])
    v = _dot(u, wv_ref[...])
    h = (g * jax.nn.sigmoid(g) * v).astype(BF16)
    o_ref[...] += _dot(h, wd_ref[...])

    @pl.when(j == n_f - 1)
    def _():
        for r0, nr in slabs:
            rows = pl.ds(r0, nr)
            y = ALPHA * x_ref[rows, :] + 0.5 * mod_ref[0, 2] * o_ref[rows, :]
            o_ref[rows, :] = _layer_norm(y, lng_ref[...], lnb_ref[...])


def _ffn(x, mod, w_g, w_v, v_col0, w_down, ln_g, ln_b, *, tm, tf, rows_per_mod, emit_bf16):
    rows = x.shape[0]
    mod_rows = mod.shape[2]
    n_f = D_FF // tf
    v_blk0 = v_col0 // tf
    assert not emit_bf16 or rows == tm
    kern = functools.partial(_ffn_kernel, slabs=_slabs(tm, mod_rows), n_f=n_f,
                             emit_bf16=emit_bf16)
    tiles_per_mod = rows_per_mod // tm
    up_spec = pl.BlockSpec((D_MODEL, tf), lambda i, j: (0, j))
    down_spec = pl.BlockSpec((tf, D_MODEL), lambda i, j: (j, 0))
    out_specs = [pl.BlockSpec((tm, D_MODEL), lambda i, j: (i, 0))]
    out_shape = [jax.ShapeDtypeStruct((rows, D_MODEL), F32)]
    if emit_bf16:
        out_specs += [up_spec, up_spec, down_spec]
        out_shape += [jax.ShapeDtypeStruct((D_MODEL, D_FF), BF16)] * 2
        out_shape += [jax.ShapeDtypeStruct((D_FF, D_MODEL), BF16)]
    return pl.pallas_call(
        kern,
        grid=(rows // tm, n_f),
        in_specs=[
            pl.BlockSpec((tm, D_MODEL), lambda i, j: (i, 0)),
            pl.BlockSpec((1, 3, mod_rows, D_MODEL), lambda i, j: (i // tiles_per_mod, 0, 0, 0)),
            up_spec,
            pl.BlockSpec((D_MODEL, tf), lambda i, j: (0, v_blk0 + j)),
            down_spec,
            pl.BlockSpec((1, D_MODEL), lambda i, j: (0, 0)),
            pl.BlockSpec((1, D_MODEL), lambda i, j: (0, 0)),
        ],
        out_specs=out_specs,
        out_shape=out_shape,
        scratch_shapes=[pltpu.VMEM((tm, D_MODEL), BF16)],
        compiler_params=_params("arbitrary", "arbitrary"),
        name="macaron_ffn",
    )(x, mod, w_g, w_v, w_down, ln_g, ln_b)


def _lru_coeffs(xc, wbd_ref, ba, bx, lam):
    xcb = xc.astype(BF16)
    ra, rx = [], []
    for q in range(D_RNN // V7X_MXU_DIM):
        cols = slice(q * V7X_MXU_DIM, (q + 1) * V7X_MXU_DIM)
        rq = _dot(xcb[:, cols], wbd_ref[q])
        ra.append(rq[:, :V7X_MXU_DIM])
        rx.append(rq[:, V7X_MXU_DIM:])
    r = jax.nn.sigmoid(jnp.concatenate(ra, axis=-1) + ba)
    i = jax.nn.sigmoid(jnp.concatenate(rx, axis=-1) + bx)
    log_a = (-LRU_C * r) * _softplus(-lam)
    a = jnp.exp(log_a)
    z = -jnp.tanh(log_a) * (a * a + 1.0)
    mult = jnp.where(z > 0.0, z * lax.rsqrt(z), 0.0)
    return a, mult, i * xc


def _prompt_mixer_kernel(x_ref, mod_ref, w_ref, cw_ref, cb_ref, wbd_ref, ba_ref, bx_ref, lam_ref,
                         glng_ref, glnb_ref, ws_ref, bs_ref,
                         ya_ref, yb_ref, ht_ref, tailo_ref,
                         tail_ref, h_ref, hs_ref, *, tt):
    t = pl.program_id(1)
    ng = tt // V7X_SUBLANES

    @pl.when(t == 0)
    def _():
        tail_ref[...] = jnp.zeros_like(tail_ref)
        h_ref[...] = jnp.zeros_like(h_ref)

    u = _modulated_bf16(x_ref, mod_ref, [(0, tt)])

    xr = _dot(u, w_ref[:, 0:D_RNN])
    prev = tail_ref[...]
    row = lax.broadcasted_iota(jnp.int32, (tt, D_RNN), 0)
    row8 = lax.broadcasted_iota(jnp.int32, (V7X_SUBLANES, D_RNN), 0)
    xc = cb_ref[...]
    for k in range(CONV_W):
        s = CONV_W - 1 - k
        if s == 0:
            xs = xr
        else:
            rolled = pltpu.roll(xr, s, 0)
            head = jnp.where(row8 < s, pltpu.roll(prev, s, 0), rolled[:V7X_SUBLANES])
            xs = jnp.concatenate([head, rolled[V7X_SUBLANES:]], axis=0)
        xc = xc + xs * cw_ref[k:k + 1, :]
    new_tail = xr[tt - V7X_SUBLANES:, :]
    tail_ref[...] = new_tail
    tailo_ref[0] = new_tail

    a, mult, gated = _lru_coeffs(xc, wbd_ref, ba_ref[...], bx_ref[...], lam_ref[...])
    mult = jnp.where(jnp.logical_and(row == 0, t == 0), 1.0, mult)
    b = mult * gated

    a = a.reshape(ng, V7X_SUBLANES, D_RNN)
    b = b.reshape(ng, V7X_SUBLANES, D_RNN)
    sub = lax.broadcasted_iota(jnp.int32, (ng, V7X_SUBLANES, D_RNN), 1)
    s = 1
    while s < V7X_SUBLANES:
        keep = sub >= s
        a_prev = jnp.where(keep, pltpu.roll(a, s, 1), 1.0)
        b_prev = jnp.where(keep, pltpu.roll(b, s, 1), 0.0)
        b = a * b_prev + b
        a = a * a_prev
        s *= 2
    h_in = h_ref[...]
    for g in range(ng):
        h = a[g] * h_in + b[g]
        hs_ref[g * V7X_SUBLANES:(g + 1) * V7X_SUBLANES, :] = h
        h_in = h[V7X_SUBLANES - 1:, :]
    h_ref[...] = h_in
    ht_ref[0] = h_in
    gr = _dot(u, w_ref[:, D_RNN:2 * D_RNN])
    ya_ref[...] = (hs_ref[...] * _gelu_tanh(gr)).astype(BF16)

    gu = _dot(u, w_ref[:, 2 * D_RNN:2 * D_RNN + D_GMLP])
    gv = _dot(u, w_ref[:, 2 * D_RNN + D_GMLP:BRANCH_COLS])
    vn = _layer_norm(gv, glng_ref[...], glnb_ref[...]).astype(BF16)
    ti = lax.broadcasted_iota(jnp.int32, (CHUNK, CHUNK), 0)
    si = lax.broadcasted_iota(jnp.int32, (CHUNK, CHUNK), 1)
    causal = si <= ti
    for g in range(GMLP_GROUPS):
        w = jnp.where(causal, ws_ref[g], 0.0).astype(BF16)
        c0 = g * GMLP_GW
        bias = bs_ref[:, c0:c0 + GMLP_GW]
        for c in range(tt // CHUNK):
            r0 = c * CHUNK
            s_blk = _dot(w, vn[r0:r0 + CHUNK, c0:c0 + GMLP_GW]) + bias
            yb_ref[r0:r0 + CHUNK, c0:c0 + GMLP_GW] = (
                gu[r0:r0 + CHUNK, c0:c0 + GMLP_GW] * s_blk).astype(BF16)


def _prompt_mixer(x, mod, w_branch, conv_w, conv_b, wbd, ba, bx, lam, gln_g, gln_b, ws, bs_tile,
                  *, batch, seq, tt):
    nt = seq // tt
    kern = functools.partial(_prompt_mixer_kernel, tt=tt)
    vec = pl.BlockSpec((1, D_RNN), lambda b, t: (0, 0))
    return pl.pallas_call(
        kern,
        grid=(batch, nt),
        in_specs=[
            pl.BlockSpec((tt, D_MODEL), lambda b, t: (b * nt + t, 0)),
            pl.BlockSpec((1, 3, 1, D_MODEL), lambda b, t: (b, 0, 0, 0)),
            pl.BlockSpec((D_MODEL, BRANCH_COLS), lambda b, t: (0, 0)),
            pl.BlockSpec((CONV_W, D_RNN), lambda b, t: (0, 0)),
            vec,
            pl.BlockSpec((D_RNN // V7X_MXU_DIM, V7X_MXU_DIM, 2 * V7X_MXU_DIM), lambda b, t: (0, 0, 0)),
            vec, vec, vec, vec, vec,
            pl.BlockSpec((GMLP_GROUPS, CHUNK, CHUNK), lambda b, t: (0, 0, 0)),
            pl.BlockSpec((CHUNK, D_GMLP), lambda b, t: (0, 0)),
        ],
        out_specs=[
            pl.BlockSpec((tt, D_RNN), lambda b, t: (b * nt + t, 0)),
            pl.BlockSpec((tt, D_GMLP), lambda b, t: (b * nt + t, 0)),
            pl.BlockSpec((1, 1, D_RNN), lambda b, t: (b, 0, 0)),
            pl.BlockSpec((1, V7X_SUBLANES, D_RNN), lambda b, t: (b, 0, 0)),
        ],
        out_shape=[
            jax.ShapeDtypeStruct((batch * seq, D_RNN), BF16),
            jax.ShapeDtypeStruct((batch * seq, D_GMLP), BF16),
            jax.ShapeDtypeStruct((batch, 1, D_RNN), F32),
            jax.ShapeDtypeStruct((batch, V7X_SUBLANES, D_RNN), F32),
        ],
        scratch_shapes=[pltpu.VMEM((V7X_SUBLANES, D_RNN), F32), pltpu.VMEM((1, D_RNN), F32),
                        pltpu.VMEM((tt, D_RNN), F32)],
        compiler_params=_params("arbitrary", "arbitrary"),
        name="prompt_mixer",
    )(x, mod, w_branch, conv_w, conv_b, wbd, ba, bx, lam, gln_g, gln_b, ws, bs_tile)


_PROJ_TN = 1024


def _sample_proj_kernel(x_ref, mod_ref, w_ref, lng_ref, lnb_ref, o_ref, wb_ref, u_ref, *, slabs):
    j = pl.program_id(0)

    @pl.when(j == 0)
    def _():
        u_ref[...] = _modulated_bf16(x_ref, mod_ref, slabs)

    wb_ref[...] = w_ref[...].astype(BF16)
    r = _dot(u_ref[...], wb_ref[...])

    @pl.when(jnp.logical_or(j == 0, j == 2))
    def _():
        o_ref[...] = r

    @pl.when(j == 1)
    def _():
        o_ref[...] = _gelu_tanh(r)

    @pl.when(j == 3)
    def _():
        o_ref[...] = _layer_norm(r, lng_ref[...], lnb_ref[...])


def _sample_proj(x, mod, w_in, gln_g, gln_b):
    rows = x.shape[0]
    mod_rows = mod.shape[2]
    kern = functools.partial(_sample_proj_kernel, slabs=_slabs(rows, mod_rows))
    w_spec = pl.BlockSpec((D_MODEL, _PROJ_TN), lambda j: (0, j))
    return pl.pallas_call(
        kern,
        grid=(BRANCH_COLS // _PROJ_TN,),
        in_specs=[
            pl.BlockSpec((rows, D_MODEL), lambda j: (0, 0)),
            pl.BlockSpec((1, 3, mod_rows, D_MODEL), lambda j: (0, 0, 0, 0)),
            w_spec,
            pl.BlockSpec((1, D_GMLP), lambda j: (0, 0)),
            pl.BlockSpec((1, D_GMLP), lambda j: (0, 0)),
        ],
        out_specs=[pl.BlockSpec((rows, _PROJ_TN), lambda j: (0, j)), w_spec],
        out_shape=[jax.ShapeDtypeStruct((rows, BRANCH_COLS), F32),
                   jax.ShapeDtypeStruct((D_MODEL, BRANCH_COLS), BF16)],
        scratch_shapes=[pltpu.VMEM((rows, D_MODEL), BF16)],
        compiler_params=_params("arbitrary"),
        name="sample_proj",
    )(x, mod, w_in, gln_g, gln_b)


def _sample_mix_kernel(xr_ref, gg_ref, gu_ref, vn_ref, sconv_ref, h0_ref, cw_ref, cb_ref,
                       wbd_ref, ba_ref, bx_ref, lam_ref, wsx_ref, bsx_ref,
                       ya_ref, yb_ref, ht_ref, xc_ref, *, nb, nt):
    def slab(t):
        return pl.ds(t * nb, nb)

    xpad = [sconv_ref[k] for k in range(CONV_W - 1)] + [xr_ref[slab(t), :] for t in range(nt)]
    for t in range(nt):
        xc = cb_ref[...]
        for k in range(CONV_W):
            xc = xc + xpad[t + k] * cw_ref[k:k + 1, :]
        xc_ref[slab(t), :] = xc

    a, mult, gated = _lru_coeffs(xc_ref[...], wbd_ref, ba_ref[...], bx_ref[...], lam_ref[...])
    b = mult * gated
    h = h0_ref[...]
    for t in range(nt):
        lo, hi = t * nb, (t + 1) * nb
        h = a[lo:hi] * h + b[lo:hi]
        ya_ref[slab(t), :] = (h * gg_ref[slab(t), :]).astype(BF16)
    ht_ref[...] = h

    for t in range(nt):
        s = bsx_ref[t:t + 1, :]
        for sp in range(t + 1):
            s = s + wsx_ref[t * nt + sp:t * nt + sp + 1, :] * vn_ref[slab(sp), :]
        yb_ref[slab(t), :] = (gu_ref[slab(t), :] * s).astype(BF16)


def _sample_mix(proj, sconv_tm, h0, conv_w, conv_b, wbd, ba, bx, lam, wsx, bsx, *, nb, nt):
    rows = nb * nt
    kern = functools.partial(_sample_mix_kernel, nb=nb, nt=nt)
    vec = pl.BlockSpec((1, D_RNN), lambda i: (0, 0))

    def col(jb):
        return pl.BlockSpec((rows, D_RNN), lambda i: (0, jb))

    return pl.pallas_call(
        kern,
        grid=(1,),
        in_specs=[
            col(0), col(1), col(2), col(3),
            pl.BlockSpec((CONV_W - 1, nb, D_RNN), lambda i: (0, 0, 0)),
            pl.BlockSpec((nb, D_RNN), lambda i: (0, 0)),
            pl.BlockSpec((CONV_W, D_RNN), lambda i: (0, 0)),
            vec,
            pl.BlockSpec((D_RNN // V7X_MXU_DIM, V7X_MXU_DIM, 2 * V7X_MXU_DIM), lambda i: (0, 0, 0)),
            vec, vec, vec,
            pl.BlockSpec((nt * nt, D_GMLP), lambda i: (0, 0)),
            pl.BlockSpec((nt, D_GMLP), lambda i: (0, 0)),
        ],
        out_specs=[
            pl.BlockSpec((rows, D_RNN), lambda i: (0, 0)),
            pl.BlockSpec((rows, D_GMLP), lambda i: (0, 0)),
            pl.BlockSpec((nb, D_RNN), lambda i: (0, 0)),
        ],
        out_shape=[
            jax.ShapeDtypeStruct((rows, D_RNN), BF16),
            jax.ShapeDtypeStruct((rows, D_GMLP), BF16),
            jax.ShapeDtypeStruct((nb, D_RNN), F32),
        ],
        scratch_shapes=[pltpu.VMEM((rows, D_RNN), F32)],
        compiler_params=_params("arbitrary"),
        name="sample_mix",
    )(proj, proj, proj, proj, sconv_tm, h0, conv_w, conv_b, wbd, ba, bx, lam, wsx, bsx)


def _merge_kernel(ya_ref, yb_ref, x_ref, mod_ref, wgate_ref, wpa_ref, wpb_ref, wo_ref,
                  lng_ref, lnb_ref, o_ref, *, slabs):
    u = _modulated_bf16(x_ref, mod_ref, slabs)
    y_a = jax.nn.sigmoid(_dot(u, wgate_ref[:, :D_MODEL])) * _dot(ya_ref[...], wpa_ref[...])
    y_b = jax.nn.sigmoid(_dot(u, wgate_ref[:, D_MODEL:])) * _dot(yb_ref[...], wpb_ref[...])
    mix = _dot((y_a + y_b).astype(BF16), wo_ref[...])
    for r0, nr in slabs:
        y = ALPHA * x_ref[r0:r0 + nr, :] + mod_ref[0, 2] * mix[r0:r0 + nr, :]
        o_ref[r0:r0 + nr, :] = _layer_norm(y, lng_ref[...], lnb_ref[...])


def _merge(ya, yb, x, mod, w_gate, w_pa, w_pb, w_out, ln_g, ln_b, *, tm, rows_per_mod):
    rows = x.shape[0]
    mod_rows = mod.shape[2]
    kern = functools.partial(_merge_kernel, slabs=_slabs(tm, mod_rows))
    tiles_per_mod = rows_per_mod // tm
    whole = lambda shape: pl.BlockSpec(shape, lambda i: (0, 0))
    return pl.pallas_call(
        kern,
        grid=(rows // tm,),
        in_specs=[
            pl.BlockSpec((tm, D_RNN), lambda i: (i, 0)),
            pl.BlockSpec((tm, D_GMLP), lambda i: (i, 0)),
            pl.BlockSpec((tm, D_MODEL), lambda i: (i, 0)),
            pl.BlockSpec((1, 3, mod_rows, D_MODEL), lambda i: (i // tiles_per_mod, 0, 0, 0)),
            whole((D_MODEL, 2 * D_MODEL)),
            whole((D_RNN, D_MODEL)),
            whole((D_GMLP, D_MODEL)),
            whole((D_MODEL, D_MODEL)),
            whole((1, D_MODEL)),
            whole((1, D_MODEL)),
        ],
        out_specs=pl.BlockSpec((tm, D_MODEL), lambda i: (i, 0)),
        out_shape=jax.ShapeDtypeStruct((rows, D_MODEL), F32),
        compiler_params=_params("arbitrary"),
        name="merge_out",
    )(ya, yb, x, mod, w_gate, w_pa, w_pb, w_out, ln_g, ln_b)


def _block_diag_tiles(w):
    per = V7X_MXU_DIM // RNN_BW
    w4 = w.reshape(RNN_BLOCKS // per, per, RNN_BW, RNN_BW)
    eye = jnp.eye(per, dtype=w.dtype)
    t = w4[:, :, :, None, :] * eye[None, :, None, :, None]
    return t.reshape(RNN_BLOCKS // per, V7X_MXU_DIM, V7X_MXU_DIM)


def kernel(x_prompt, x_sample, state_conv, state_h, c_prompt, c_sample, w_ada, b_ada, ffn1_w_gu, ffn1_w_down, ffn2_w_gu, ffn2_w_down, w_in, conv_w, conv_b, lru_wa, lru_ba, lru_wx, lru_bx, lru_lambda, gmlp_ln_g, gmlp_ln_b, gmlp_ws, gmlp_bs, w_pa, w_pb, w_out, ln_g, ln_b):
    assert w_ada.shape[0] == DEPTH == 1
    bp, seq, _ = x_prompt.shape
    bs, dseq, _ = x_sample.shape
    l = 0

    w_gate = w_in[l, :, BRANCH_COLS:].astype(BF16)
    w_pa_b, w_pb_b, w_out_b = w_pa[l].astype(BF16), w_pb[l].astype(BF16), w_out[l].astype(BF16)
    wbd = jnp.concatenate([_block_diag_tiles(lru_wa[l]), _block_diag_tiles(lru_wx[l])],
                          axis=-1).astype(BF16)

    row = lambda v: v.reshape(1, -1)
    cb, ba, bx, lam = row(conv_b[l]), row(lru_ba[l]), row(lru_bx[l]), row(lru_lambda[l])
    gln_g, gln_b = row(gmlp_ln_g[l]), row(gmlp_ln_b[l])
    lng = [row(ln_g[l, k]) for k in range(3)]
    lnb = [row(ln_b[l, k]) for k in range(3)]

    bs_tile = jnp.repeat(gmlp_bs[l].T, GMLP_GW, axis=1)
    wsx = jnp.repeat(jnp.transpose(gmlp_ws[l][:, :dseq, :dseq], (1, 2, 0)), GMLP_GW, axis=2)
    wsx = wsx.reshape(dseq * dseq, D_GMLP)
    bsx = bs_tile[:dseq]

    c_all = jnp.concatenate([c_prompt, c_sample], axis=0)
    mod = _modulation(c_all, w_ada[l], row(b_ada[l]))
    mod_p = mod[:bp].reshape(bp, N_MOD, 1, D_MODEL)
    mod_s = jnp.transpose(mod[bp:].reshape(bs, N_MOD, D_MODEL), (1, 0, 2))[None]

    xp = x_prompt.reshape(bp * seq, D_MODEL)
    xs = jnp.transpose(x_sample, (1, 0, 2)).reshape(dseq * bs, D_MODEL)
    n_s = dseq * bs
    keep = CONV_W - 1

    def ffn_pair(x_s, x_p, k, w_gu, w_down):
        y_s, w_g_b, w_v_b, w_d_b = _ffn(
            x_s, mod_s[:, 3 * k:3 * k + 3], w_gu[l], w_gu[l], D_FF, w_down[l], lng[k], lnb[k],
            tm=n_s, tf=512, rows_per_mod=n_s, emit_bf16=True)
        (y_p,) = _ffn(
            x_p, mod_p[:, 3 * k:3 * k + 3], w_g_b, w_v_b, 0, w_d_b, lng[k], lnb[k],
            tm=1024, tf=256, rows_per_mod=seq, emit_bf16=False)
        return y_s, y_p

    def merge(ya, yb, x1, modx, rows_per_mod):
        return _merge(ya, yb, x1, modx[:, 3:6], w_gate, w_pa_b, w_pb_b, w_out_b, lng[1], lnb[1],
                      tm=256, rows_per_mod=rows_per_mod)

    x1s, x1p = ffn_pair(xs, xp, 0, ffn1_w_gu, ffn1_w_down)

    proj_s, w_branch = _sample_proj(x1s, mod_s[:, 3:6], w_in[l], gln_g, gln_b)
    sconv_tm = jnp.transpose(state_conv[l], (1, 0, 2))
    ya_s, yb_s, ht_s = _sample_mix(proj_s, sconv_tm, state_h[l], conv_w[l], cb, wbd, ba, bx, lam,
                                   wsx, bsx, nb=bs, nt=dseq)
    ya_p, yb_p, ht_p, tail_p = _prompt_mixer(
        x1p, mod_p[:, 3:6], w_branch, conv_w[l], cb, wbd, ba, bx, lam, gln_g, gln_b,
        gmlp_ws[l], bs_tile, batch=bp, seq=seq, tt=512)
    x2s = merge(ya_s, yb_s, x1s, mod_s, n_s)
    x2p = merge(ya_p, yb_p, x1p, mod_p, seq)

    x3s, x3p = ffn_pair(x2s, x2p, 2, ffn2_w_gu, ffn2_w_down)

    y_prompt = x3p.reshape(bp, seq, D_MODEL)
    y_sample = jnp.transpose(x3s.reshape(dseq, bs, D_MODEL), (1, 0, 2))
    new_conv_p = tail_p[:, V7X_SUBLANES - keep:][None]
    new_h_p = ht_p.reshape(bp, D_RNN)[None]
    xr_s = lax.slice(proj_s.reshape(dseq, bs, BRANCH_COLS), (dseq - keep, 0, 0), (dseq, bs, D_RNN))
    new_conv_s = jnp.transpose(xr_s, (1, 0, 2))[None]
    new_h_s = ht_s[None]
    vn_s = lax.slice(proj_s.reshape(dseq, bs, BRANCH_COLS), (0, 0, 2 * D_RNN + D_GMLP),
                     (dseq, bs, BRANCH_COLS))
    new_v_s = jnp.transpose(vn_s, (1, 0, 2))[None]
    return (y_prompt, y_sample, new_conv_p, new_h_p, new_conv_s, new_h_s, new_v_s)
```

```python
import functools
import math

import jax
import jax.numpy as jnp
from jax import lax
from jax.experimental import pallas as pl
from jax.experimental.pallas import tpu as pltpu

D_MODEL = 2048
D_RNN = 1024
RNN_BLOCKS = 16
RNN_BW = D_RNN // RNN_BLOCKS
CONV_W = 4
LRU_C = 8.0
D_GMLP = 1024
CHUNK = 128
GMLP_GROUPS = 8
GMLP_GW = D_GMLP // GMLP_GROUPS
D_FF = 5632
N_MOD = 9
IN_COLS = 2 * D_RNN + 2 * D_GMLP + 2 * D_MODEL
BRANCH_COLS = 2 * D_RNN + 2 * D_GMLP
LN_EPS = 1e-5
DEPTH = 1
ALPHA = (2.0 * DEPTH) ** 0.25

V7X_SUBLANES = 8
V7X_LANES = 128
V7X_MXU_DIM = 256
V7X_VMEM_BYTES = 64 * 1024 * 1024
V7X_VMEM_LIMIT_BYTES = 58 * 1024 * 1024
V7X_VMEM_FFN_LIMIT_BYTES = V7X_VMEM_BYTES - 512 * 1024

BF16 = jnp.bfloat16
F32 = jnp.float32


def _dot(a, b):
    return jnp.dot(a, b, preferred_element_type=F32)


def _layer_norm(x, g, b):
    mu = jnp.mean(x, axis=-1, keepdims=True)
    xc = x - mu
    var = jnp.mean(xc * xc, axis=-1, keepdims=True)
    return xc * lax.rsqrt(var + LN_EPS) * g + b


def _gelu_tanh(x):
    c = math.sqrt(2.0 / math.pi)
    return 0.5 * x * (1.0 + jnp.tanh(c * (x + 0.044715 * (x * x * x))))


def _softplus(x):
    return jnp.maximum(x, 0.0) + jnp.log1p(jnp.exp(-jnp.abs(x)))


def _params(*semantics, vmem_limit_bytes=V7X_VMEM_LIMIT_BYTES):
    return pltpu.CompilerParams(dimension_semantics=semantics, vmem_limit_bytes=vmem_limit_bytes)


def _slabs(tm, mod_rows):
    if mod_rows == 1:
        return [(0, tm)]
    return [(s, mod_rows) for s in range(0, tm, mod_rows)]


def _modulated_bf16(x_ref, mod_ref, slabs):
    parts = [(x_ref[r0:r0 + nr, :] * (1.0 + mod_ref[0, 1]) + mod_ref[0, 0]).astype(BF16)
             for r0, nr in slabs]
    return parts[0] if len(parts) == 1 else jnp.concatenate(parts, axis=0)


def _mod_kernel(c_ref, w_ref, b_ref, o_ref):
    c = c_ref[...]
    a = (c * jax.nn.sigmoid(c)).astype(BF16)
    o_ref[...] = _dot(a, w_ref[...].astype(BF16)) + b_ref[...]


def _modulation(c_all, w_ada, b_ada, tn=1024):
    rows = c_all.shape[0]
    n = w_ada.shape[1]
    return pl.pallas_call(
        _mod_kernel,
        grid=(n // tn,),
        in_specs=[
            pl.BlockSpec((rows, D_MODEL), lambda j: (0, 0)),
            pl.BlockSpec((D_MODEL, tn), lambda j: (0, j)),
            pl.BlockSpec((1, tn), lambda j: (0, j)),
        ],
        out_specs=pl.BlockSpec((rows, tn), lambda j: (0, j)),
        out_shape=jax.ShapeDtypeStruct((rows, n), F32),
        compiler_params=_params("arbitrary"),
        name="adaln_mod",
    )(c_all, w_ada, b_ada)


def _ffn_kernel(x_ref, mod_ref, wg_ref, wv_ref, wd_ref, lng_ref, lnb_ref, o_ref, *rest,
                slabs, n_f, emit_bf16):
    u_ref = rest[-1]
    j = pl.program_id(1)

    @pl.when(j == 0)
    def _():
        u_ref[...] = _modulated_bf16(x_ref, mod_ref, slabs)
        o_ref[...] = jnp.zeros_like(o_ref)

    if emit_bf16:
        wg, wv, wd = (r[...].astype(BF16) for r in (wg_ref, wv_ref, wd_ref))
        for dst, w in zip(rest[:3], (wg, wv, wd)):
            dst[...] = w
    else:
        wg, wv, wd = wg_ref[...], wv_ref[...], wd_ref[...]
    u = u_ref[...]
    g = _dot(u, wg)
    v = _dot(u, wv)
    h = (g * jax.nn.sigmoid(g) * v).astype(BF16)
    o_ref[...] += _dot(h, wd)

    @pl.when(j == n_f - 1)
    def _():
        for r0, nr in slabs:
            rows = pl.ds(r0, nr)
            y = ALPHA * x_ref[rows, :] + 0.5 * mod_ref[0, 2] * o_ref[rows, :]
            o_ref[rows, :] = _layer_norm(y, lng_ref[...], lnb_ref[...])


def _ffn(x, mod, w_g, w_v, v_col0, w_down, ln_g, ln_b, *, tm, tf, rows_per_mod, emit_bf16):
    rows = x.shape[0]
    mod_rows = mod.shape[2]
    n_f = D_FF // tf
    v_blk0 = v_col0 // tf
    kern = functools.partial(_ffn_kernel, slabs=_slabs(tm, mod_rows), n_f=n_f,
                             emit_bf16=emit_bf16)
    tiles_per_mod = rows_per_mod // tm
    up_spec = pl.BlockSpec((D_MODEL, tf), lambda i, j: (0, j))
    down_spec = pl.BlockSpec((tf, D_MODEL), lambda i, j: (j, 0))
    out_specs = [pl.BlockSpec((tm, D_MODEL), lambda i, j: (i, 0))]
    out_shape = [jax.ShapeDtypeStruct((rows, D_MODEL), F32)]
    if emit_bf16:
        emit_blk = lambda i, j: jnp.where(i == 0, j, n_f)
        out_specs += [pl.BlockSpec((D_MODEL, tf), lambda i, j: (0, emit_blk(i, j)))] * 2
        out_specs += [pl.BlockSpec((tf, D_MODEL), lambda i, j: (emit_blk(i, j), 0))]
        out_shape += [jax.ShapeDtypeStruct((D_MODEL, D_FF + tf), BF16)] * 2
        out_shape += [jax.ShapeDtypeStruct((D_FF + tf, D_MODEL), BF16)]
    return pl.pallas_call(
        kern,
        grid=(rows // tm, n_f),
        in_specs=[
            pl.BlockSpec((tm, D_MODEL), lambda i, j: (i, 0)),
            pl.BlockSpec((1, 3, mod_rows, D_MODEL), lambda i, j: (i // tiles_per_mod, 0, 0, 0)),
            up_spec,
            pl.BlockSpec((D_MODEL, tf), lambda i, j: (0, v_blk0 + j)),
            down_spec,
            pl.BlockSpec((1, D_MODEL), lambda i, j: (0, 0)),
            pl.BlockSpec((1, D_MODEL), lambda i, j: (0, 0)),
        ],
        out_specs=out_specs,
        out_shape=out_shape,
        scratch_shapes=[pltpu.VMEM((tm, D_MODEL), BF16)],
        compiler_params=_params("arbitrary", "arbitrary", vmem_limit_bytes=V7X_VMEM_FFN_LIMIT_BYTES),
        name="macaron_ffn",
    )(x, mod, w_g, w_v, w_down, ln_g, ln_b)


def _lru_coeffs(xc, wbd_ref, ba, bx, lam):
    xcb = xc.astype(BF16)
    ra, rx = [], []
    for q in range(D_RNN // V7X_MXU_DIM):
        cols = slice(q * V7X_MXU_DIM, (q + 1) * V7X_MXU_DIM)
        rq = _dot(xcb[:, cols], wbd_ref[q])
        ra.append(rq[:, :V7X_MXU_DIM])
        rx.append(rq[:, V7X_MXU_DIM:])
    r = jax.nn.sigmoid(jnp.concatenate(ra, axis=-1) + ba)
    i = jax.nn.sigmoid(jnp.concatenate(rx, axis=-1) + bx)
    log_a = (-LRU_C * r) * _softplus(-lam)
    a = jnp.exp(log_a)
    z = -jnp.tanh(log_a) * (a * a + 1.0)
    mult = jnp.where(z > 0.0, z * lax.rsqrt(z), 0.0)
    return a, mult, i * xc


def _prompt_mixer_kernel(x_ref, mod_ref, w_ref, cw_ref, cb_ref, wbd_ref, ba_ref, bx_ref, lam_ref,
                         glng_ref, glnb_ref, ws_ref, bs_ref,
                         ya_ref, yb_ref, ht_ref, tailo_ref,
                         tail_ref, h_ref, hs_ref, *, tt):
    t = pl.program_id(1)
    ng = tt // V7X_SUBLANES

    @pl.when(t == 0)
    def _():
        tail_ref[...] = jnp.zeros_like(tail_ref)
        h_ref[...] = jnp.zeros_like(h_ref)

    u = _modulated_bf16(x_ref, mod_ref, [(0, tt)])

    xr = _dot(u, w_ref[:, 0:D_RNN])
    prev = tail_ref[...]
    row = lax.broadcasted_iota(jnp.int32, (tt, D_RNN), 0)
    row8 = lax.broadcasted_iota(jnp.int32, (V7X_SUBLANES, D_RNN), 0)
    xc = cb_ref[...]
    for k in range(CONV_W):
        s = CONV_W - 1 - k
        if s == 0:
            xs = xr
        else:
            rolled = pltpu.roll(xr, s, 0)
            head = jnp.where(row8 < s, pltpu.roll(prev, s, 0), rolled[:V7X_SUBLANES])
            xs = jnp.concatenate([head, rolled[V7X_SUBLANES:]], axis=0)
        xc = xc + xs * cw_ref[k:k + 1, :]
    new_tail = xr[tt - V7X_SUBLANES:, :]
    tail_ref[...] = new_tail
    tailo_ref[0] = new_tail

    a, mult, gated = _lru_coeffs(xc, wbd_ref, ba_ref[...], bx_ref[...], lam_ref[...])
    mult = jnp.where(jnp.logical_and(row == 0, t == 0), 1.0, mult)
    b = mult * gated

    a = a.reshape(ng, V7X_SUBLANES, D_RNN)
    b = b.reshape(ng, V7X_SUBLANES, D_RNN)
    sub = lax.broadcasted_iota(jnp.int32, (ng, V7X_SUBLANES, D_RNN), 1)
    s = 1
    while s < V7X_SUBLANES:
        keep = sub >= s
        a_prev = jnp.where(keep, pltpu.roll(a, s, 1), 1.0)
        b_prev = jnp.where(keep, pltpu.roll(b, s, 1), 0.0)
        b = a * b_prev + b
        a = a * a_prev
        s *= 2
    h_in = h_ref[...]
    for g in range(ng):
        h = a[g] * h_in + b[g]
        hs_ref[g * V7X_SUBLANES:(g + 1) * V7X_SUBLANES, :] = h
        h_in = h[V7X_SUBLANES - 1:, :]
    h_ref[...] = h_in
    ht_ref[0] = h_in
    gr = _dot(u, w_ref[:, D_RNN:2 * D_RNN])
    ya_ref[...] = (hs_ref[...] * _gelu_tanh(gr)).astype(BF16)

    gu = _dot(u, w_ref[:, 2 * D_RNN:2 * D_RNN + D_GMLP])
    gv = _dot(u, w_ref[:, 2 * D_RNN + D_GMLP:BRANCH_COLS])
    vn = _layer_norm(gv, glng_ref[...], glnb_ref[...]).astype(BF16)
    ti = lax.broadcasted_iota(jnp.int32, (CHUNK, CHUNK), 0)
    si = lax.broadcasted_iota(jnp.int32, (CHUNK, CHUNK), 1)
    causal = si <= ti
    for g in range(GMLP_GROUPS):
        w = jnp.where(causal, ws_ref[g], 0.0).astype(BF16)
        c0 = g * GMLP_GW
        bias = bs_ref[:, c0:c0 + GMLP_GW]
        n_chunks = tt // CHUNK
        rhs = jnp.concatenate([vn[c * CHUNK:(c + 1) * CHUNK, c0:c0 + GMLP_GW]
                               for c in range(n_chunks)], axis=1)
        s_all = _dot(w, rhs)
        for c in range(n_chunks):
            r0 = c * CHUNK
            s_blk = s_all[:, c * GMLP_GW:(c + 1) * GMLP_GW] + bias
            yb_ref[r0:r0 + CHUNK, c0:c0 + GMLP_GW] = (
                gu[r0:r0 + CHUNK, c0:c0 + GMLP_GW] * s_blk).astype(BF16)


def _prompt_mixer(x, mod, w_branch, conv_w, conv_b, wbd, ba, bx, lam, gln_g, gln_b, ws, bs_tile,
                  *, batch, seq, tt):
    nt = seq // tt
    kern = functools.partial(_prompt_mixer_kernel, tt=tt)
    vec = pl.BlockSpec((1, D_RNN), lambda b, t: (0, 0))
    return pl.pallas_call(
        kern,
        grid=(batch, nt),
        in_specs=[
            pl.BlockSpec((tt, D_MODEL), lambda b, t: (b * nt + t, 0)),
            pl.BlockSpec((1, 3, 1, D_MODEL), lambda b, t: (b, 0, 0, 0)),
            pl.BlockSpec((D_MODEL, BRANCH_COLS), lambda b, t: (0, 0)),
            pl.BlockSpec((CONV_W, D_RNN), lambda b, t: (0, 0)),
            vec,
            pl.BlockSpec((D_RNN // V7X_MXU_DIM, V7X_MXU_DIM, 2 * V7X_MXU_DIM), lambda b, t: (0, 0, 0)),
            vec, vec, vec, vec, vec,
            pl.BlockSpec((GMLP_GROUPS, CHUNK, CHUNK), lambda b, t: (0, 0, 0)),
            pl.BlockSpec((CHUNK, D_GMLP), lambda b, t: (0, 0)),
        ],
        out_specs=[
            pl.BlockSpec((tt, D_RNN), lambda b, t: (b * nt + t, 0)),
            pl.BlockSpec((tt, D_GMLP), lambda b, t: (b * nt + t, 0)),
            pl.BlockSpec((1, 1, D_RNN), lambda b, t: (b, 0, 0)),
            pl.BlockSpec((1, V7X_SUBLANES, D_RNN), lambda b, t: (b, 0, 0)),
        ],
        out_shape=[
            jax.ShapeDtypeStruct((batch * seq, D_RNN), BF16),
            jax.ShapeDtypeStruct((batch * seq, D_GMLP), BF16),
            jax.ShapeDtypeStruct((batch, 1, D_RNN), F32),
            jax.ShapeDtypeStruct((batch, V7X_SUBLANES, D_RNN), F32),
        ],
        scratch_shapes=[pltpu.VMEM((V7X_SUBLANES, D_RNN), F32), pltpu.VMEM((1, D_RNN), F32),
                        pltpu.VMEM((tt, D_RNN), F32)],
        compiler_params=_params("arbitrary", "arbitrary"),
        name="prompt_mixer",
    )(x, mod, w_branch, conv_w, conv_b, wbd, ba, bx, lam, gln_g, gln_b, ws, bs_tile)


_PROJ_TN = 1024


def _sample_proj_kernel(x_ref, mod_ref, w_ref, lng_ref, lnb_ref, o_ref, wb_ref, u_ref, *, slabs):
    j = pl.program_id(0)

    @pl.when(j == 0)
    def _():
        u_ref[...] = _modulated_bf16(x_ref, mod_ref, slabs)

    wb_ref[...] = w_ref[...].astype(BF16)
    r = _dot(u_ref[...], wb_ref[...])

    @pl.when(jnp.logical_or(j == 0, j == 2))
    def _():
        o_ref[...] = r

    @pl.when(j == 1)
    def _():
        o_ref[...] = _gelu_tanh(r)

    @pl.when(j == 3)
    def _():
        o_ref[...] = _layer_norm(r, lng_ref[...], lnb_ref[...])


def _sample_proj(x, mod, w_in, gln_g, gln_b):
    rows = x.shape[0]
    mod_rows = mod.shape[2]
    kern = functools.partial(_sample_proj_kernel, slabs=_slabs(rows, mod_rows))
    w_spec = pl.BlockSpec((D_MODEL, _PROJ_TN), lambda j: (0, j))
    return pl.pallas_call(
        kern,
        grid=(BRANCH_COLS // _PROJ_TN,),
        in_specs=[
            pl.BlockSpec((rows, D_MODEL), lambda j: (0, 0)),
            pl.BlockSpec((1, 3, mod_rows, D_MODEL), lambda j: (0, 0, 0, 0)),
            w_spec,
            pl.BlockSpec((1, D_GMLP), lambda j: (0, 0)),
            pl.BlockSpec((1, D_GMLP), lambda j: (0, 0)),
        ],
        out_specs=[pl.BlockSpec((rows, _PROJ_TN), lambda j: (0, j)), w_spec],
        out_shape=[jax.ShapeDtypeStruct((rows, BRANCH_COLS), F32),
                   jax.ShapeDtypeStruct((D_MODEL, BRANCH_COLS), BF16)],
        scratch_shapes=[pltpu.VMEM((rows, D_MODEL), BF16)],
        compiler_params=_params("arbitrary"),
        name="sample_proj",
    )(x, mod, w_in, gln_g, gln_b)


def _sample_mix_kernel(xr_ref, gg_ref, gu_ref, vn_ref, sconv_ref, h0_ref, cw_ref, cb_ref,
                       wbd_ref, ba_ref, bx_ref, lam_ref, wsx_ref, bsx_ref,
                       ya_ref, yb_ref, ht_ref, xc_ref, *, nb, nt):
    def slab(t):
        return pl.ds(t * nb, nb)

    xpad = [sconv_ref[k] for k in range(CONV_W - 1)] + [xr_ref[slab(t), :] for t in range(nt)]
    for t in range(nt):
        xc = cb_ref[...]
        for k in range(CONV_W):
            xc = xc + xpad[t + k] * cw_ref[k:k + 1, :]
        xc_ref[slab(t), :] = xc

    a, mult, gated = _lru_coeffs(xc_ref[...], wbd_ref, ba_ref[...], bx_ref[...], lam_ref[...])
    b = mult * gated
    h = h0_ref[...]
    for t in range(nt):
        lo, hi = t * nb, (t + 1) * nb
        h = a[lo:hi] * h + b[lo:hi]
        ya_ref[slab(t), :] = (h * gg_ref[slab(t), :]).astype(BF16)
    ht_ref[...] = h

    for t in range(nt):
        s = bsx_ref[t:t + 1, :]
        for sp in range(t + 1):
            s = s + wsx_ref[t * nt + sp:t * nt + sp + 1, :] * vn_ref[slab(sp), :]
        yb_ref[slab(t), :] = (gu_ref[slab(t), :] * s).astype(BF16)


def _sample_mix(proj, sconv_tm, h0, conv_w, conv_b, wbd, ba, bx, lam, wsx, bsx, *, nb, nt):
    rows = nb * nt
    kern = functools.partial(_sample_mix_kernel, nb=nb, nt=nt)
    vec = pl.BlockSpec((1, D_RNN), lambda i: (0, 0))

    def col(jb):
        return pl.BlockSpec((rows, D_RNN), lambda i: (0, jb))

    return pl.pallas_call(
        kern,
        grid=(1,),
        in_specs=[
            col(0), col(1), col(2), col(3),
            pl.BlockSpec((CONV_W - 1, nb, D_RNN), lambda i: (0, 0, 0)),
            pl.BlockSpec((nb, D_RNN), lambda i: (0, 0)),
            pl.BlockSpec((CONV_W, D_RNN), lambda i: (0, 0)),
            vec,
            pl.BlockSpec((D_RNN // V7X_MXU_DIM, V7X_MXU_DIM, 2 * V7X_MXU_DIM), lambda i: (0, 0, 0)),
            vec, vec, vec,
            pl.BlockSpec((nt * nt, D_GMLP), lambda i: (0, 0)),
            pl.BlockSpec((nt, D_GMLP), lambda i: (0, 0)),
        ],
        out_specs=[
            pl.BlockSpec((rows, D_RNN), lambda i: (0, 0)),
            pl.BlockSpec((rows, D_GMLP), lambda i: (0, 0)),
            pl.BlockSpec((nb, D_RNN), lambda i: (0, 0)),
        ],
        out_shape=[
            jax.ShapeDtypeStruct((rows, D_RNN), BF16),
            jax.ShapeDtypeStruct((rows, D_GMLP), BF16),
            jax.ShapeDtypeStruct((nb, D_RNN), F32),
        ],
        scratch_shapes=[pltpu.VMEM((rows, D_RNN), F32)],
        compiler_params=_params("arbitrary"),
        name="sample_mix",
    )(proj, proj, proj, proj, sconv_tm, h0, conv_w, conv_b, wbd, ba, bx, lam, wsx, bsx)


def _merge_kernel(ya_ref, yb_ref, x_ref, mod_ref, wgate_ref, wpa_ref, wpb_ref, wo_ref,
                  lng_ref, lnb_ref, o_ref, *, slabs):
    u = _modulated_bf16(x_ref, mod_ref, slabs)
    y_a = jax.nn.sigmoid(_dot(u, wgate_ref[:, :D_MODEL])) * _dot(ya_ref[...], wpa_ref[...])
    y_b = jax.nn.sigmoid(_dot(u, wgate_ref[:, D_MODEL:])) * _dot(yb_ref[...], wpb_ref[...])
    mix = _dot((y_a + y_b).astype(BF16), wo_ref[...])
    for r0, nr in slabs:
        y = ALPHA * x_ref[r0:r0 + nr, :] + mod_ref[0, 2] * mix[r0:r0 + nr, :]
        o_ref[r0:r0 + nr, :] = _layer_norm(y, lng_ref[...], lnb_ref[...])


def _merge(ya, yb, x, mod, w_gate, w_pa, w_pb, w_out, ln_g, ln_b, *, tm, rows_per_mod):
    rows = x.shape[0]
    mod_rows = mod.shape[2]
    kern = functools.partial(_merge_kernel, slabs=_slabs(tm, mod_rows))
    tiles_per_mod = rows_per_mod // tm
    whole = lambda shape: pl.BlockSpec(shape, lambda i: (0, 0))
    return pl.pallas_call(
        kern,
        grid=(rows // tm,),
        in_specs=[
            pl.BlockSpec((tm, D_RNN), lambda i: (i, 0)),
            pl.BlockSpec((tm, D_GMLP), lambda i: (i, 0)),
            pl.BlockSpec((tm, D_MODEL), lambda i: (i, 0)),
            pl.BlockSpec((1, 3, mod_rows, D_MODEL), lambda i: (i // tiles_per_mod, 0, 0, 0)),
            whole((D_MODEL, 2 * D_MODEL)),
            whole((D_RNN, D_MODEL)),
            whole((D_GMLP, D_MODEL)),
            whole((D_MODEL, D_MODEL)),
            whole((1, D_MODEL)),
            whole((1, D_MODEL)),
        ],
        out_specs=pl.BlockSpec((tm, D_MODEL), lambda i: (i, 0)),
        out_shape=jax.ShapeDtypeStruct((rows, D_MODEL), F32),
        compiler_params=_params("arbitrary"),
        name="merge_out",
    )(ya, yb, x, mod, w_gate, w_pa, w_pb, w_out, ln_g, ln_b)


def _block_diag_tiles(w):
    per = V7X_MXU_DIM // RNN_BW
    w4 = w.reshape(RNN_BLOCKS // per, per, RNN_BW, RNN_BW)
    eye = jnp.eye(per, dtype=w.dtype)
    t = w4[:, :, :, None, :] * eye[None, :, None, :, None]
    return t.reshape(RNN_BLOCKS // per, V7X_MXU_DIM, V7X_MXU_DIM)


def kernel(x_prompt, x_sample, state_conv, state_h, c_prompt, c_sample, w_ada, b_ada, ffn1_w_gu, ffn1_w_down, ffn2_w_gu, ffn2_w_down, w_in, conv_w, conv_b, lru_wa, lru_ba, lru_wx, lru_bx, lru_lambda, gmlp_ln_g, gmlp_ln_b, gmlp_ws, gmlp_bs, w_pa, w_pb, w_out, ln_g, ln_b):
    assert w_ada.shape[0] == DEPTH == 1
    bp, seq, _ = x_prompt.shape
    bs, dseq, _ = x_sample.shape
    l = 0

    w_gate = w_in[l, :, BRANCH_COLS:].astype(BF16)
    w_pa_b, w_pb_b, w_out_b = w_pa[l].astype(BF16), w_pb[l].astype(BF16), w_out[l].astype(BF16)
    wbd = jnp.concatenate([_block_diag_tiles(lru_wa[l]), _block_diag_tiles(lru_wx[l])],
                          axis=-1).astype(BF16)

    row = lambda v: v.reshape(1, -1)
    cb, ba, bx, lam = row(conv_b[l]), row(lru_ba[l]), row(lru_bx[l]), row(lru_lambda[l])
    gln_g, gln_b = row(gmlp_ln_g[l]), row(gmlp_ln_b[l])
    lng = [row(ln_g[l, k]) for k in range(3)]
    lnb = [row(ln_b[l, k]) for k in range(3)]

    bs_tile = jnp.repeat(gmlp_bs[l].T, GMLP_GW, axis=1)
    wsx = jnp.repeat(jnp.transpose(gmlp_ws[l][:, :dseq, :dseq], (1, 2, 0)), GMLP_GW, axis=2)
    wsx = wsx.reshape(dseq * dseq, D_GMLP)
    bsx = bs_tile[:dseq]

    c_all = jnp.concatenate([c_prompt, c_sample], axis=0)
    mod = _modulation(c_all, w_ada[l], row(b_ada[l]))
    mod_p = mod[:bp].reshape(bp, N_MOD, 1, D_MODEL)
    mod_s = jnp.transpose(mod[bp:].reshape(bs, N_MOD, D_MODEL), (1, 0, 2))[None]

    xp = x_prompt.reshape(bp * seq, D_MODEL)
    xs = jnp.transpose(x_sample, (1, 0, 2)).reshape(dseq * bs, D_MODEL)
    n_s = dseq * bs
    keep = CONV_W - 1

    def ffn_pair(x_s, x_p, k, w_gu, w_down):
        y_p, w_g_b, w_v_b, w_d_b = _ffn(
            x_p, mod_p[:, 3 * k:3 * k + 3], w_gu[l], w_gu[l], D_FF, w_down[l], lng[k], lnb[k],
            tm=1024, tf=256, rows_per_mod=seq, emit_bf16=True)
        (y_s,) = _ffn(
            x_s, mod_s[:, 3 * k:3 * k + 3], w_g_b, w_v_b, 0, w_d_b, lng[k], lnb[k],
            tm=n_s, tf=512, rows_per_mod=n_s, emit_bf16=False)
        return y_s, y_p

    def merge(ya, yb, x1, modx, rows_per_mod):
        return _merge(ya, yb, x1, modx[:, 3:6], w_gate, w_pa_b, w_pb_b, w_out_b, lng[1], lnb[1],
                      tm=256, rows_per_mod=rows_per_mod)

    x1s, x1p = ffn_pair(xs, xp, 0, ffn1_w_gu, ffn1_w_down)

    proj_s, w_branch = _sample_proj(x1s, mod_s[:, 3:6], w_in[l], gln_g, gln_b)
    sconv_tm = jnp.transpose(state_conv[l], (1, 0, 2))
    ya_s, yb_s, ht_s = _sample_mix(proj_s, sconv_tm, state_h[l], conv_w[l], cb, wbd, ba, bx, lam,
                                   wsx, bsx, nb=bs, nt=dseq)
    ya_p, yb_p, ht_p, tail_p = _prompt_mixer(
        x1p, mod_p[:, 3:6], w_branch, conv_w[l], cb, wbd, ba, bx, lam, gln_g, gln_b,
        gmlp_ws[l], bs_tile, batch=bp, seq=seq, tt=512)
    x2s = merge(ya_s, yb_s, x1s, mod_s, n_s)
    x2p = merge(ya_p, yb_p, x1p, mod_p, seq)

    x3s, x3p = ffn_pair(x2s, x2p, 2, ffn2_w_gu, ffn2_w_down)

    y_prompt = x3p.reshape(bp, seq, D_MODEL)
    y_sample = jnp.transpose(x3s.reshape(dseq, bs, D_MODEL), (1, 0, 2))
    new_conv_p = tail_p[:, V7X_SUBLANES - keep:][None]
    new_h_p = ht_p.reshape(bp, D_RNN)[None]
    xr_s = lax.slice(proj_s.reshape(dseq, bs, BRANCH_COLS), (dseq - keep, 0, 0), (dseq, bs, D_RNN))
    new_conv_s = jnp.transpose(xr_s, (1, 0, 2))[None]
    new_h_s = ht_s[None]
    vn_s = lax.slice(proj_s.reshape(dseq, bs, BRANCH_COLS), (0, 0, 2 * D_RNN + D_GMLP),
                     (dseq, bs, BRANCH_COLS))
    new_v_s = jnp.transpose(vn_s, (1, 0, 2))[None]
    return (y_prompt, y_sample, new_conv_p, new_h_p, new_conv_s, new_h_s, new_v_s)
```

```python
import functools
import math

import jax
import jax.numpy as jnp
from jax import lax
from jax.experimental import pallas as pl
from jax.experimental.pallas import tpu as pltpu

D_MODEL = 2048
D_RNN = 1024
RNN_BLOCKS = 16
RNN_BW = D_RNN // RNN_BLOCKS
CONV_W = 4
LRU_C = 8.0
D_GMLP = 1024
CHUNK = 128
GMLP_GROUPS = 8
GMLP_GW = D_GMLP // GMLP_GROUPS
D_FF = 5632
N_MOD = 9
IN_COLS = 2 * D_RNN + 2 * D_GMLP + 2 * D_MODEL
BRANCH_COLS = 2 * D_RNN + 2 * D_GMLP
LN_EPS = 1e-5
DEPTH = 1
ALPHA = (2.0 * DEPTH) ** 0.25

V7X_SUBLANES = 8
V7X_LANES = 128
V7X_MXU_DIM = 256
V7X_VMEM_BYTES = 64 * 1024 * 1024
V7X_VMEM_LIMIT_BYTES = 58 * 1024 * 1024
V7X_VMEM_FFN_LIMIT_BYTES = V7X_VMEM_BYTES - 512 * 1024

BF16 = jnp.bfloat16
F32 = jnp.float32


def _dot(a, b):
    return jnp.dot(a, b, preferred_element_type=F32)


def _layer_norm(x, g, b):
    mu = jnp.mean(x, axis=-1, keepdims=True)
    xc = x - mu
    var = jnp.mean(xc * xc, axis=-1, keepdims=True)
    return xc * lax.rsqrt(var + LN_EPS) * g + b


def _gelu_tanh(x):
    c = math.sqrt(2.0 / math.pi)
    return 0.5 * x * (1.0 + jnp.tanh(c * (x + 0.044715 * (x * x * x))))


def _softplus(x):
    return jnp.maximum(x, 0.0) + jnp.log1p(jnp.exp(-jnp.abs(x)))


def _params(*semantics, vmem_limit_bytes=V7X_VMEM_LIMIT_BYTES):
    return pltpu.CompilerParams(dimension_semantics=semantics, vmem_limit_bytes=vmem_limit_bytes)


def _slabs(tm, mod_rows):
    if mod_rows == 1:
        return [(0, tm)]
    return [(s, mod_rows) for s in range(0, tm, mod_rows)]


def _modulated_bf16(x_ref, mod_ref, slabs):
    parts = [(x_ref[r0:r0 + nr, :] * (1.0 + mod_ref[0, 1]) + mod_ref[0, 0]).astype(BF16)
             for r0, nr in slabs]
    return parts[0] if len(parts) == 1 else jnp.concatenate(parts, axis=0)


def _mod_kernel(c_ref, w_ref, b_ref, o_ref):
    c = c_ref[...]
    a = (c * jax.nn.sigmoid(c)).astype(BF16)
    o_ref[...] = _dot(a, w_ref[...].astype(BF16)) + b_ref[...]


def _modulation(c_all, w_ada, b_ada, tn=1024):
    rows = c_all.shape[0]
    n = w_ada.shape[1]
    return pl.pallas_call(
        _mod_kernel,
        grid=(n // tn,),
        in_specs=[
            pl.BlockSpec((rows, D_MODEL), lambda j: (0, 0)),
            pl.BlockSpec((D_MODEL, tn), lambda j: (0, j)),
            pl.BlockSpec((1, tn), lambda j: (0, j)),
        ],
        out_specs=pl.BlockSpec((rows, tn), lambda j: (0, j)),
        out_shape=jax.ShapeDtypeStruct((rows, n), F32),
        compiler_params=_params("arbitrary"),
        name="adaln_mod",
    )(c_all, w_ada, b_ada)


def _ffn_kernel(x_ref, mod_ref, wg_ref, wv_ref, wd_ref, lng_ref, lnb_ref, o_ref, *rest,
                slabs, n_f, emit_bf16):
    u_ref = rest[-1]
    j = pl.program_id(1)

    def swiglu_down():
        if emit_bf16:
            wg, wv, wd = (r[...].astype(BF16) for r in (wg_ref, wv_ref, wd_ref))
            for dst, w in zip(rest[:3], (wg, wv, wd)):
                dst[...] = w
        else:
            wg, wv, wd = wg_ref[...], wv_ref[...], wd_ref[...]
        u = u_ref[...]
        g = _dot(u, wg)
        v = _dot(u, wv)
        h = (g * jax.nn.sigmoid(g) * v).astype(BF16)
        return _dot(h, wd)

    @pl.when(j == 0)
    def _():
        u_ref[...] = _modulated_bf16(x_ref, mod_ref, slabs)
        o_ref[...] = swiglu_down()

    @pl.when(jnp.logical_and(j > 0, j < n_f - 1))
    def _():
        o_ref[...] += swiglu_down()

    @pl.when(j == n_f - 1)
    def _():
        o_ref[...] += swiglu_down()
        for r0, nr in slabs:
            rows = pl.ds(r0, nr)
            y = ALPHA * x_ref[rows, :] + 0.5 * mod_ref[0, 2] * o_ref[rows, :]
            o_ref[rows, :] = _layer_norm(y, lng_ref[...], lnb_ref[...])


def _ffn(x, mod, w_g, w_v, v_col0, w_down, ln_g, ln_b, *, tm, tf, rows_per_mod, emit_bf16):
    rows = x.shape[0]
    mod_rows = mod.shape[2]
    n_f = D_FF // tf
    v_blk0 = v_col0 // tf
    kern = functools.partial(_ffn_kernel, slabs=_slabs(tm, mod_rows), n_f=n_f,
                             emit_bf16=emit_bf16)
    tiles_per_mod = rows_per_mod // tm
    up_spec = pl.BlockSpec((D_MODEL, tf), lambda i, j: (0, j))
    down_spec = pl.BlockSpec((tf, D_MODEL), lambda i, j: (j, 0))
    out_specs = [pl.BlockSpec((tm, D_MODEL), lambda i, j: (i, 0))]
    out_shape = [jax.ShapeDtypeStruct((rows, D_MODEL), F32)]
    if emit_bf16:
        emit_blk = lambda i, j: jnp.where(i == 0, j, n_f)
        out_specs += [pl.BlockSpec((D_MODEL, tf), lambda i, j: (0, emit_blk(i, j)))] * 2
        out_specs += [pl.BlockSpec((tf, D_MODEL), lambda i, j: (emit_blk(i, j), 0))]
        out_shape += [jax.ShapeDtypeStruct((D_MODEL, D_FF + tf), BF16)] * 2
        out_shape += [jax.ShapeDtypeStruct((D_FF + tf, D_MODEL), BF16)]
    return pl.pallas_call(
        kern,
        grid=(rows // tm, n_f),
        in_specs=[
            pl.BlockSpec((tm, D_MODEL), lambda i, j: (i, 0)),
            pl.BlockSpec((1, 3, mod_rows, D_MODEL), lambda i, j: (i // tiles_per_mod, 0, 0, 0)),
            up_spec,
            pl.BlockSpec((D_MODEL, tf), lambda i, j: (0, v_blk0 + j)),
            down_spec,
            pl.BlockSpec((1, D_MODEL), lambda i, j: (0, 0)),
            pl.BlockSpec((1, D_MODEL), lambda i, j: (0, 0)),
        ],
        out_specs=out_specs,
        out_shape=out_shape,
        scratch_shapes=[pltpu.VMEM((tm, D_MODEL), BF16)],
        compiler_params=_params("arbitrary", "arbitrary", vmem_limit_bytes=V7X_VMEM_FFN_LIMIT_BYTES),
        name="macaron_ffn",
    )(x, mod, w_g, w_v, w_down, ln_g, ln_b)


def _lru_coeffs(xc, wbd_ref, ba, bx, lam):
    xcb = xc.astype(BF16)
    ra, rx = [], []
    for q in range(D_RNN // V7X_MXU_DIM):
        cols = slice(q * V7X_MXU_DIM, (q + 1) * V7X_MXU_DIM)
        rq = _dot(xcb[:, cols], wbd_ref[q])
        ra.append(rq[:, :V7X_MXU_DIM])
        rx.append(rq[:, V7X_MXU_DIM:])
    r = jax.nn.sigmoid(jnp.concatenate(ra, axis=-1) + ba)
    i = jax.nn.sigmoid(jnp.concatenate(rx, axis=-1) + bx)
    log_a = (-LRU_C * r) * _softplus(-lam)
    a = jnp.exp(log_a)
    z = -jnp.tanh(log_a) * (a * a + 1.0)
    mult = jnp.where(z > 0.0, z * lax.rsqrt(z), 0.0)
    return a, mult, i * xc


def _prompt_mixer_kernel(x_ref, mod_ref, w_ref, cw_ref, cb_ref, wbd_ref, ba_ref, bx_ref, lam_ref,
                         glng_ref, glnb_ref, ws_ref, bs_ref,
                         ya_ref, yb_ref, ht_ref, tailo_ref,
                         tail_ref, h_ref, hs_ref, *, tt):
    t = pl.program_id(1)
    ng = tt // V7X_SUBLANES

    @pl.when(t == 0)
    def _():
        tail_ref[...] = jnp.zeros_like(tail_ref)
        h_ref[...] = jnp.zeros_like(h_ref)

    u = _modulated_bf16(x_ref, mod_ref, [(0, tt)])

    xr = _dot(u, w_ref[:, 0:D_RNN])
    prev = tail_ref[...]
    row = lax.broadcasted_iota(jnp.int32, (tt, D_RNN), 0)
    row8 = lax.broadcasted_iota(jnp.int32, (V7X_SUBLANES, D_RNN), 0)
    xc = cb_ref[...]
    for k in range(CONV_W):
        s = CONV_W - 1 - k
        if s == 0:
            xs = xr
        else:
            rolled = pltpu.roll(xr, s, 0)
            head = jnp.where(row8 < s, pltpu.roll(prev, s, 0), rolled[:V7X_SUBLANES])
            xs = jnp.concatenate([head, rolled[V7X_SUBLANES:]], axis=0)
        xc = xc + xs * cw_ref[k:k + 1, :]
    new_tail = xr[tt - V7X_SUBLANES:, :]
    tail_ref[...] = new_tail
    tailo_ref[0] = new_tail

    a, mult, gated = _lru_coeffs(xc, wbd_ref, ba_ref[...], bx_ref[...], lam_ref[...])
    mult = jnp.where(jnp.logical_and(row == 0, t == 0), 1.0, mult)
    b = mult * gated

    a = a.reshape(ng, V7X_SUBLANES, D_RNN)
    b = b.reshape(ng, V7X_SUBLANES, D_RNN)
    sub = lax.broadcasted_iota(jnp.int32, (ng, V7X_SUBLANES, D_RNN), 1)
    s = 1
    while s < V7X_SUBLANES:
        keep = sub >= s
        a_prev = jnp.where(keep, pltpu.roll(a, s, 1), 1.0)
        b_prev = jnp.where(keep, pltpu.roll(b, s, 1), 0.0)
        b = a * b_prev + b
        a = a * a_prev
        s *= 2
    h_in = h_ref[...]
    for g in range(ng):
        h = a[g] * h_in + b[g]
        hs_ref[g * V7X_SUBLANES:(g + 1) * V7X_SUBLANES, :] = h
        h_in = h[V7X_SUBLANES - 1:, :]
    h_ref[...] = h_in
    ht_ref[0] = h_in
    gr = _dot(u, w_ref[:, D_RNN:2 * D_RNN])
    ya_ref[...] = (hs_ref[...] * _gelu_tanh(gr)).astype(BF16)

    gu = _dot(u, w_ref[:, 2 * D_RNN:2 * D_RNN + D_GMLP])
    gv = _dot(u, w_ref[:, 2 * D_RNN + D_GMLP:BRANCH_COLS])
    vn = _layer_norm(gv, glng_ref[...], glnb_ref[...]).astype(BF16)
    ti = lax.broadcasted_iota(jnp.int32, (CHUNK, CHUNK), 0)
    si = lax.broadcasted_iota(jnp.int32, (CHUNK, CHUNK), 1)
    causal = si <= ti
    for g in range(GMLP_GROUPS):
        w = jnp.where(causal, ws_ref[g], 0.0).astype(BF16)
        c0 = g * GMLP_GW
        bias = bs_ref[:, c0:c0 + GMLP_GW]
        n_chunks = tt // CHUNK
        rhs = jnp.concatenate([vn[c * CHUNK:(c + 1) * CHUNK, c0:c0 + GMLP_GW]
                               for c in range(n_chunks)], axis=1)
        s_all = _dot(w, rhs)
        for c in range(n_chunks):
            r0 = c * CHUNK
            s_blk = s_all[:, c * GMLP_GW:(c + 1) * GMLP_GW] + bias
            yb_ref[r0:r0 + CHUNK, c0:c0 + GMLP_GW] = (
                gu[r0:r0 + CHUNK, c0:c0 + GMLP_GW] * s_blk).astype(BF16)


def _prompt_mixer(x, mod, w_branch, conv_w, conv_b, wbd, ba, bx, lam, gln_g, gln_b, ws, bs_tile,
                  *, batch, seq, tt):
    nt = seq // tt
    kern = functools.partial(_prompt_mixer_kernel, tt=tt)
    vec = pl.BlockSpec((1, D_RNN), lambda b, t: (0, 0))
    return pl.pallas_call(
        kern,
        grid=(batch, nt),
        in_specs=[
            pl.BlockSpec((tt, D_MODEL), lambda b, t: (b * nt + t, 0)),
            pl.BlockSpec((1, 3, 1, D_MODEL), lambda b, t: (b, 0, 0, 0)),
            pl.BlockSpec((D_MODEL, BRANCH_COLS), lambda b, t: (0, 0)),
            pl.BlockSpec((CONV_W, D_RNN), lambda b, t: (0, 0)),
            vec,
            pl.BlockSpec((D_RNN // V7X_MXU_DIM, V7X_MXU_DIM, 2 * V7X_MXU_DIM), lambda b, t: (0, 0, 0)),
            vec, vec, vec, vec, vec,
            pl.BlockSpec((GMLP_GROUPS, CHUNK, CHUNK), lambda b, t: (0, 0, 0)),
            pl.BlockSpec((CHUNK, D_GMLP), lambda b, t: (0, 0)),
        ],
        out_specs=[
            pl.BlockSpec((tt, D_RNN), lambda b, t: (b * nt + t, 0)),
            pl.BlockSpec((tt, D_GMLP), lambda b, t: (b * nt + t, 0)),
            pl.BlockSpec((1, 1, D_RNN), lambda b, t: (b, 0, 0)),
            pl.BlockSpec((1, V7X_SUBLANES, D_RNN), lambda b, t: (b, 0, 0)),
        ],
        out_shape=[
            jax.ShapeDtypeStruct((batch * seq, D_RNN), BF16),
            jax.ShapeDtypeStruct((batch * seq, D_GMLP), BF16),
            jax.ShapeDtypeStruct((batch, 1, D_RNN), F32),
            jax.ShapeDtypeStruct((batch, V7X_SUBLANES, D_RNN), F32),
        ],
        scratch_shapes=[pltpu.VMEM((V7X_SUBLANES, D_RNN), F32), pltpu.VMEM((1, D_RNN), F32),
                        pltpu.VMEM((tt, D_RNN), F32)],
        compiler_params=_params("arbitrary", "arbitrary"),
        name="prompt_mixer",
    )(x, mod, w_branch, conv_w, conv_b, wbd, ba, bx, lam, gln_g, gln_b, ws, bs_tile)


_PROJ_TN = 1024


def _sample_proj_kernel(x_ref, mod_ref, w_ref, lng_ref, lnb_ref, o_ref, wb_ref, u_ref, *, slabs):
    j = pl.program_id(0)

    @pl.when(j == 0)
    def _():
        u_ref[...] = _modulated_bf16(x_ref, mod_ref, slabs)

    wb_ref[...] = w_ref[...].astype(BF16)
    r = _dot(u_ref[...], wb_ref[...])

    @pl.when(jnp.logical_or(j == 0, j == 2))
    def _():
        o_ref[...] = r

    @pl.when(j == 1)
    def _():
        o_ref[...] = _gelu_tanh(r)

    @pl.when(j == 3)
    def _():
        o_ref[...] = _layer_norm(r, lng_ref[...], lnb_ref[...])


def _sample_proj(x, mod, w_in, gln_g, gln_b):
    rows = x.shape[0]
    mod_rows = mod.shape[2]
    kern = functools.partial(_sample_proj_kernel, slabs=_slabs(rows, mod_rows))
    w_spec = pl.BlockSpec((D_MODEL, _PROJ_TN), lambda j: (0, j))
    return pl.pallas_call(
        kern,
        grid=(BRANCH_COLS // _PROJ_TN,),
        in_specs=[
            pl.BlockSpec((rows, D_MODEL), lambda j: (0, 0)),
            pl.BlockSpec((1, 3, mod_rows, D_MODEL), lambda j: (0, 0, 0, 0)),
            w_spec,
            pl.BlockSpec((1, D_GMLP), lambda j: (0, 0)),
            pl.BlockSpec((1, D_GMLP), lambda j: (0, 0)),
        ],
        out_specs=[pl.BlockSpec((rows, _PROJ_TN), lambda j: (0, j)), w_spec],
        out_shape=[jax.ShapeDtypeStruct((rows, BRANCH_COLS), F32),
                   jax.ShapeDtypeStruct((D_MODEL, BRANCH_COLS), BF16)],
        scratch_shapes=[pltpu.VMEM((rows, D_MODEL), BF16)],
        compiler_params=_params("arbitrary"),
        name="sample_proj",
    )(x, mod, w_in, gln_g, gln_b)


def _sample_mix_kernel(xr_ref, gg_ref, gu_ref, vn_ref, sconv_ref, h0_ref, cw_ref, cb_ref,
                       wbd_ref, ba_ref, bx_ref, lam_ref, wsx_ref, bsx_ref,
                       ya_ref, yb_ref, ht_ref, xc_ref, *, nb, nt):
    def slab(t):
        return pl.ds(t * nb, nb)

    xpad = [sconv_ref[k] for k in range(CONV_W - 1)] + [xr_ref[slab(t), :] for t in range(nt)]
    for t in range(nt):
        xc = cb_ref[...]
        for k in range(CONV_W):
            xc = xc + xpad[t + k] * cw_ref[k:k + 1, :]
        xc_ref[slab(t), :] = xc

    a, mult, gated = _lru_coeffs(xc_ref[...], wbd_ref, ba_ref[...], bx_ref[...], lam_ref[...])
    b = mult * gated
    h = h0_ref[...]
    for t in range(nt):
        lo, hi = t * nb, (t + 1) * nb
        h = a[lo:hi] * h + b[lo:hi]
        ya_ref[slab(t), :] = (h * gg_ref[slab(t), :]).astype(BF16)
    ht_ref[...] = h

    for t in range(nt):
        s = bsx_ref[t:t + 1, :]
        for sp in range(t + 1):
            s = s + wsx_ref[t * nt + sp:t * nt + sp + 1, :] * vn_ref[slab(sp), :]
        yb_ref[slab(t), :] = (gu_ref[slab(t), :] * s).astype(BF16)


def _sample_mix(proj, sconv_tm, h0, conv_w, conv_b, wbd, ba, bx, lam, wsx, bsx, *, nb, nt):
    rows = nb * nt
    kern = functools.partial(_sample_mix_kernel, nb=nb, nt=nt)
    vec = pl.BlockSpec((1, D_RNN), lambda i: (0, 0))

    def col(jb):
        return pl.BlockSpec((rows, D_RNN), lambda i: (0, jb))

    return pl.pallas_call(
        kern,
        grid=(1,),
        in_specs=[
            col(0), col(1), col(2), col(3),
            pl.BlockSpec((CONV_W - 1, nb, D_RNN), lambda i: (0, 0, 0)),
            pl.BlockSpec((nb, D_RNN), lambda i: (0, 0)),
            pl.BlockSpec((CONV_W, D_RNN), lambda i: (0, 0)),
            vec,
            pl.BlockSpec((D_RNN // V7X_MXU_DIM, V7X_MXU_DIM, 2 * V7X_MXU_DIM), lambda i: (0, 0, 0)),
            vec, vec, vec,
            pl.BlockSpec((nt * nt, D_GMLP), lambda i: (0, 0)),
            pl.BlockSpec((nt, D_GMLP), lambda i: (0, 0)),
        ],
        out_specs=[
            pl.BlockSpec((rows, D_RNN), lambda i: (0, 0)),
            pl.BlockSpec((rows, D_GMLP), lambda i: (0, 0)),
            pl.BlockSpec((nb, D_RNN), lambda i: (0, 0)),
        ],
        out_shape=[
            jax.ShapeDtypeStruct((rows, D_RNN), BF16),
            jax.ShapeDtypeStruct((rows, D_GMLP), BF16),
            jax.ShapeDtypeStruct((nb, D_RNN), F32),
        ],
        scratch_shapes=[pltpu.VMEM((rows, D_RNN), F32)],
        compiler_params=_params("arbitrary"),
        name="sample_mix",
    )(proj, proj, proj, proj, sconv_tm, h0, conv_w, conv_b, wbd, ba, bx, lam, wsx, bsx)


def _merge_kernel(ya_ref, yb_ref, x_ref, mod_ref, wgate_ref, wpa_ref, wpb_ref, wo_ref,
                  lng_ref, lnb_ref, o_ref, *, slabs):
    u = _modulated_bf16(x_ref, mod_ref, slabs)
    y_a = jax.nn.sigmoid(_dot(u, wgate_ref[:, :D_MODEL])) * _dot(ya_ref[...], wpa_ref[...])
    y_b = jax.nn.sigmoid(_dot(u, wgate_ref[:, D_MODEL:])) * _dot(yb_ref[...], wpb_ref[...])
    mix = _dot((y_a + y_b).astype(BF16), wo_ref[...])
    for r0, nr in slabs:
        y = ALPHA * x_ref[r0:r0 + nr, :] + mod_ref[0, 2] * mix[r0:r0 + nr, :]
        o_ref[r0:r0 + nr, :] = _layer_norm(y, lng_ref[...], lnb_ref[...])


def _merge(ya, yb, x, mod, w_gate, w_pa, w_pb, w_out, ln_g, ln_b, *, tm, rows_per_mod):
    rows = x.shape[0]
    mod_rows = mod.shape[2]
    kern = functools.partial(_merge_kernel, slabs=_slabs(tm, mod_rows))
    tiles_per_mod = rows_per_mod // tm
    whole = lambda shape: pl.BlockSpec(shape, lambda i: (0, 0))
    return pl.pallas_call(
        kern,
        grid=(rows // tm,),
        in_specs=[
            pl.BlockSpec((tm, D_RNN), lambda i: (i, 0)),
            pl.BlockSpec((tm, D_GMLP), lambda i: (i, 0)),
            pl.BlockSpec((tm, D_MODEL), lambda i: (i, 0)),
            pl.BlockSpec((1, 3, mod_rows, D_MODEL), lambda i: (i // tiles_per_mod, 0, 0, 0)),
            whole((D_MODEL, 2 * D_MODEL)),
            whole((D_RNN, D_MODEL)),
            whole((D_GMLP, D_MODEL)),
            whole((D_MODEL, D_MODEL)),
            whole((1, D_MODEL)),
            whole((1, D_MODEL)),
        ],
        out_specs=pl.BlockSpec((tm, D_MODEL), lambda i: (i, 0)),
        out_shape=jax.ShapeDtypeStruct((rows, D_MODEL), F32),
        compiler_params=_params("arbitrary"),
        name="merge_out",
    )(ya, yb, x, mod, w_gate, w_pa, w_pb, w_out, ln_g, ln_b)


def _block_diag_tiles(w):
    per = V7X_MXU_DIM // RNN_BW
    w4 = w.reshape(RNN_BLOCKS // per, per, RNN_BW, RNN_BW)
    eye = jnp.eye(per, dtype=w.dtype)
    t = w4[:, :, :, None, :] * eye[None, :, None, :, None]
    return t.reshape(RNN_BLOCKS // per, V7X_MXU_DIM, V7X_MXU_DIM)


def kernel(x_prompt, x_sample, state_conv, state_h, c_prompt, c_sample, w_ada, b_ada, ffn1_w_gu, ffn1_w_down, ffn2_w_gu, ffn2_w_down, w_in, conv_w, conv_b, lru_wa, lru_ba, lru_wx, lru_bx, lru_lambda, gmlp_ln_g, gmlp_ln_b, gmlp_ws, gmlp_bs, w_pa, w_pb, w_out, ln_g, ln_b):
    assert w_ada.shape[0] == DEPTH == 1
    bp, seq, _ = x_prompt.shape
    bs, dseq, _ = x_sample.shape
    l = 0

    w_gate = w_in[l, :, BRANCH_COLS:].astype(BF16)
    w_pa_b, w_pb_b, w_out_b = w_pa[l].astype(BF16), w_pb[l].astype(BF16), w_out[l].astype(BF16)
    wbd = jnp.concatenate([_block_diag_tiles(lru_wa[l]), _block_diag_tiles(lru_wx[l])],
                          axis=-1).astype(BF16)

    row = lambda v: v.reshape(1, -1)
    cb, ba, bx, lam = row(conv_b[l]), row(lru_ba[l]), row(lru_bx[l]), row(lru_lambda[l])
    gln_g, gln_b = row(gmlp_ln_g[l]), row(gmlp_ln_b[l])
    lng = [row(ln_g[l, k]) for k in range(3)]
    lnb = [row(ln_b[l, k]) for k in range(3)]

    bs_tile = jnp.repeat(gmlp_bs[l].T, GMLP_GW, axis=1)
    wsx = jnp.repeat(jnp.transpose(gmlp_ws[l][:, :dseq, :dseq], (1, 2, 0)), GMLP_GW, axis=2)
    wsx = wsx.reshape(dseq * dseq, D_GMLP)
    bsx = bs_tile[:dseq]

    c_all = jnp.concatenate([c_prompt, c_sample], axis=0)
    mod = _modulation(c_all, w_ada[l], row(b_ada[l]))
    mod_p = mod[:bp].reshape(bp, N_MOD, 1, D_MODEL)
    mod_s = jnp.transpose(mod[bp:].reshape(bs, N_MOD, D_MODEL), (1, 0, 2))[None]

    xp = x_prompt.reshape(bp * seq, D_MODEL)
    xs = jnp.transpose(x_sample, (1, 0, 2)).reshape(dseq * bs, D_MODEL)
    n_s = dseq * bs
    keep = CONV_W - 1

    def ffn_pair(x_s, x_p, k, w_gu, w_down):
        y_p, w_g_b, w_v_b, w_d_b = _ffn(
            x_p, mod_p[:, 3 * k:3 * k + 3], w_gu[l], w_gu[l], D_FF, w_down[l], lng[k], lnb[k],
            tm=1024, tf=256, rows_per_mod=seq, emit_bf16=True)
        (y_s,) = _ffn(
            x_s, mod_s[:, 3 * k:3 * k + 3], w_g_b, w_v_b, 0, w_d_b, lng[k], lnb[k],
            tm=n_s, tf=512, rows_per_mod=n_s, emit_bf16=False)
        return y_s, y_p

    def merge(ya, yb, x1, modx, rows_per_mod, tm):
        return _merge(ya, yb, x1, modx[:, 3:6], w_gate, w_pa_b, w_pb_b, w_out_b, lng[1], lnb[1],
                      tm=tm, rows_per_mod=rows_per_mod)

    x1s, x1p = ffn_pair(xs, xp, 0, ffn1_w_gu, ffn1_w_down)

    proj_s, w_branch = _sample_proj(x1s, mod_s[:, 3:6], w_in[l], gln_g, gln_b)
    sconv_tm = jnp.transpose(state_conv[l], (1, 0, 2))
    ya_s, yb_s, ht_s = _sample_mix(proj_s, sconv_tm, state_h[l], conv_w[l], cb, wbd, ba, bx, lam,
                                   wsx, bsx, nb=bs, nt=dseq)
    ya_p, yb_p, ht_p, tail_p = _prompt_mixer(
        x1p, mod_p[:, 3:6], w_branch, conv_w[l], cb, wbd, ba, bx, lam, gln_g, gln_b,
        gmlp_ws[l], bs_tile, batch=bp, seq=seq, tt=512)
    x2s = merge(ya_s, yb_s, x1s, mod_s, n_s, 256)
    x2p = merge(ya_p, yb_p, x1p, mod_p, seq, 256)

    x3s, x3p = ffn_pair(x2s, x2p, 2, ffn2_w_gu, ffn2_w_down)

    y_prompt = x3p.reshape(bp, seq, D_MODEL)
    y_sample = jnp.transpose(x3s.reshape(dseq, bs, D_MODEL), (1, 0, 2))
    new_conv_p = tail_p[:, V7X_SUBLANES - keep:][None]
    new_h_p = ht_p.reshape(bp, D_RNN)[None]
    xr_s = lax.slice(proj_s.reshape(dseq, bs, BRANCH_COLS), (dseq - keep, 0, 0), (dseq, bs, D_RNN))
    new_conv_s = jnp.transpose(xr_s, (1, 0, 2))[None]
    new_h_s = ht_s[None]
    vn_s = lax.slice(proj_s.reshape(dseq, bs, BRANCH_COLS), (0, 0, 2 * D_RNN + D_GMLP),
                     (dseq, bs, BRANCH_COLS))
    new_v_s = jnp.transpose(vn_s, (1, 0, 2))[None]
    return (y_prompt, y_sample, new_conv_p, new_h_p, new_conv_s, new_h_s, new_v_s)
```

```python
import functools
import math

import jax
import jax.numpy as jnp
from jax import lax
from jax.experimental import pallas as pl
from jax.experimental.pallas import tpu as pltpu

D_MODEL = 2048
D_RNN = 1024
RNN_BLOCKS = 16
RNN_BW = D_RNN // RNN_BLOCKS
CONV_W = 4
LRU_C = 8.0
D_GMLP = 1024
CHUNK = 128
GMLP_GROUPS = 8
GMLP_GW = D_GMLP // GMLP_GROUPS
D_FF = 5632
N_MOD = 9
IN_COLS = 2 * D_RNN + 2 * D_GMLP + 2 * D_MODEL
BRANCH_COLS = 2 * D_RNN + 2 * D_GMLP
LN_EPS = 1e-5
DEPTH = 1
ALPHA = (2.0 * DEPTH) ** 0.25

V7X_SUBLANES = 8
V7X_LANES = 128
V7X_MXU_DIM = 256
V7X_VMEM_BYTES = 64 * 1024 * 1024
V7X_VMEM_LIMIT_BYTES = 58 * 1024 * 1024
V7X_VMEM_FFN_LIMIT_BYTES = V7X_VMEM_BYTES - 512 * 1024

BF16 = jnp.bfloat16
F32 = jnp.float32


def _dot(a, b):
    return jnp.dot(a, b, preferred_element_type=F32)


def _layer_norm(x, g, b):
    mu = jnp.mean(x, axis=-1, keepdims=True)
    xc = x - mu
    var = jnp.mean(xc * xc, axis=-1, keepdims=True)
    return xc * lax.rsqrt(var + LN_EPS) * g + b


def _gelu_tanh(x):
    c = math.sqrt(2.0 / math.pi)
    return 0.5 * x * (1.0 + jnp.tanh(c * (x + 0.044715 * (x * x * x))))


def _softplus(x):
    return jnp.maximum(x, 0.0) + jnp.log1p(jnp.exp(-jnp.abs(x)))


def _params(*semantics, vmem_limit_bytes=V7X_VMEM_LIMIT_BYTES):
    return pltpu.CompilerParams(dimension_semantics=semantics, vmem_limit_bytes=vmem_limit_bytes)


def _slabs(tm, mod_rows):
    if mod_rows == 1:
        return [(0, tm)]
    return [(s, mod_rows) for s in range(0, tm, mod_rows)]


def _modulated_bf16(x_ref, mod_ref, slabs):
    parts = [(x_ref[r0:r0 + nr, :] * (1.0 + mod_ref[0, 1]) + mod_ref[0, 0]).astype(BF16)
             for r0, nr in slabs]
    return parts[0] if len(parts) == 1 else jnp.concatenate(parts, axis=0)


def _mod_kernel(c_ref, w_ref, b_ref, o_ref):
    c = c_ref[...]
    a = (c * jax.nn.sigmoid(c)).astype(BF16)
    o_ref[...] = _dot(a, w_ref[...].astype(BF16)) + b_ref[...]


def _modulation(c_all, w_ada, b_ada, tn=2048):
    rows = c_all.shape[0]
    n = w_ada.shape[1]
    return pl.pallas_call(
        _mod_kernel,
        grid=(n // tn,),
        in_specs=[
            pl.BlockSpec((rows, D_MODEL), lambda j: (0, 0)),
            pl.BlockSpec((D_MODEL, tn), lambda j: (0, j)),
            pl.BlockSpec((1, tn), lambda j: (0, j)),
        ],
        out_specs=pl.BlockSpec((rows, tn), lambda j: (0, j)),
        out_shape=jax.ShapeDtypeStruct((rows, n), F32),
        compiler_params=_params("arbitrary"),
        name="adaln_mod",
    )(c_all, w_ada, b_ada)


def _ffn_kernel(x_ref, mod_ref, wg_ref, wv_ref, wd_ref, lng_ref, lnb_ref, o_ref, *rest,
                slabs, n_f, emit_bf16):
    u_ref = rest[-1]
    j = pl.program_id(1)

    def swiglu_down():
        if emit_bf16:
            wg, wv, wd = (r[...].astype(BF16) for r in (wg_ref, wv_ref, wd_ref))
            for dst, w in zip(rest[:3], (wg, wv, wd)):
                dst[...] = w
        else:
            wg, wv, wd = wg_ref[...], wv_ref[...], wd_ref[...]
        u = u_ref[...]
        g = _dot(u, wg)
        v = _dot(u, wv)
        h = (g * jax.nn.sigmoid(g) * v).astype(BF16)
        return _dot(h, wd)

    @pl.when(j == 0)
    def _():
        u_ref[...] = _modulated_bf16(x_ref, mod_ref, slabs)
        o_ref[...] = swiglu_down()

    @pl.when(jnp.logical_and(j > 0, j < n_f - 1))
    def _():
        o_ref[...] += swiglu_down()

    @pl.when(j == n_f - 1)
    def _():
        o_ref[...] += swiglu_down()
        for r0, nr in slabs:
            rows = pl.ds(r0, nr)
            y = ALPHA * x_ref[rows, :] + 0.5 * mod_ref[0, 2] * o_ref[rows, :]
            o_ref[rows, :] = _layer_norm(y, lng_ref[...], lnb_ref[...])


def _ffn(x, mod, w_g, w_v, v_col0, w_down, ln_g, ln_b, *, tm, tf, rows_per_mod, emit_bf16):
    rows = x.shape[0]
    mod_rows = mod.shape[2]
    n_f = D_FF // tf
    v_blk0 = v_col0 // tf
    kern = functools.partial(_ffn_kernel, slabs=_slabs(tm, mod_rows), n_f=n_f,
                             emit_bf16=emit_bf16)
    tiles_per_mod = rows_per_mod // tm
    up_spec = pl.BlockSpec((D_MODEL, tf), lambda i, j: (0, j))
    down_spec = pl.BlockSpec((tf, D_MODEL), lambda i, j: (j, 0))
    out_specs = [pl.BlockSpec((tm, D_MODEL), lambda i, j: (i, 0))]
    out_shape = [jax.ShapeDtypeStruct((rows, D_MODEL), F32)]
    if emit_bf16:
        pad = tf if rows > tm else 0
        emit_blk = lambda i, j: jnp.where(i == 0, j, n_f)
        out_specs += [pl.BlockSpec((D_MODEL, tf), lambda i, j: (0, emit_blk(i, j)))] * 2
        out_specs += [pl.BlockSpec((tf, D_MODEL), lambda i, j: (emit_blk(i, j), 0))]
        out_shape += [jax.ShapeDtypeStruct((D_MODEL, D_FF + pad), BF16)] * 2
        out_shape += [jax.ShapeDtypeStruct((D_FF + pad, D_MODEL), BF16)]
    return pl.pallas_call(
        kern,
        grid=(rows // tm, n_f),
        in_specs=[
            pl.BlockSpec((tm, D_MODEL), lambda i, j: (i, 0)),
            pl.BlockSpec((1, 3, mod_rows, D_MODEL), lambda i, j: (i // tiles_per_mod, 0, 0, 0)),
            up_spec,
            pl.BlockSpec((D_MODEL, tf), lambda i, j: (0, v_blk0 + j)),
            down_spec,
            pl.BlockSpec((1, D_MODEL), lambda i, j: (0, 0)),
            pl.BlockSpec((1, D_MODEL), lambda i, j: (0, 0)),
        ],
        out_specs=out_specs,
        out_shape=out_shape,
        scratch_shapes=[pltpu.VMEM((tm, D_MODEL), BF16)],
        compiler_params=_params("arbitrary", "arbitrary", vmem_limit_bytes=V7X_VMEM_FFN_LIMIT_BYTES),
        name="macaron_ffn",
    )(x, mod, w_g, w_v, w_down, ln_g, ln_b)


def _lru_coeffs(xc, wbd_ref, ba, bx, lam):
    xcb = xc.astype(BF16)
    ra, rx = [], []
    for q in range(D_RNN // V7X_MXU_DIM):
        cols = slice(q * V7X_MXU_DIM, (q + 1) * V7X_MXU_DIM)
        rq = _dot(xcb[:, cols], wbd_ref[q])
        ra.append(rq[:, :V7X_MXU_DIM])
        rx.append(rq[:, V7X_MXU_DIM:])
    r = jax.nn.sigmoid(jnp.concatenate(ra, axis=-1) + ba)
    i = jax.nn.sigmoid(jnp.concatenate(rx, axis=-1) + bx)
    log_a = (-LRU_C * r) * _softplus(-lam)
    a = jnp.exp(log_a)
    z = -jnp.tanh(log_a) * (a * a + 1.0)
    mult = jnp.where(z > 0.0, z * lax.rsqrt(z), 0.0)
    return a, mult, i * xc


def _prompt_mixer_kernel(x_ref, mod_ref, w_ref, cw_ref, cb_ref, wbd_ref, ba_ref, bx_ref, lam_ref,
                         glng_ref, glnb_ref, ws_ref, bs_ref,
                         ya_ref, yb_ref, ht_ref, tailo_ref,
                         tail_ref, h_ref, hs_ref, *, tt):
    t = pl.program_id(1)
    ng = tt // V7X_SUBLANES

    @pl.when(t == 0)
    def _():
        tail_ref[...] = jnp.zeros_like(tail_ref)
        h_ref[...] = jnp.zeros_like(h_ref)

    u = _modulated_bf16(x_ref, mod_ref, [(0, tt)])

    xr = _dot(u, w_ref[:, 0:D_RNN])
    prev = tail_ref[...]
    row = lax.broadcasted_iota(jnp.int32, (tt, D_RNN), 0)
    row8 = lax.broadcasted_iota(jnp.int32, (V7X_SUBLANES, D_RNN), 0)
    xc = cb_ref[...]
    for k in range(CONV_W):
        s = CONV_W - 1 - k
        if s == 0:
            xs = xr
        else:
            rolled = pltpu.roll(xr, s, 0)
            head = jnp.where(row8 < s, pltpu.roll(prev, s, 0), rolled[:V7X_SUBLANES])
            xs = jnp.concatenate([head, rolled[V7X_SUBLANES:]], axis=0)
        xc = xc + xs * cw_ref[k:k + 1, :]
    new_tail = xr[tt - V7X_SUBLANES:, :]
    tail_ref[...] = new_tail
    tailo_ref[0] = new_tail

    a, mult, gated = _lru_coeffs(xc, wbd_ref, ba_ref[...], bx_ref[...], lam_ref[...])
    mult = jnp.where(jnp.logical_and(row == 0, t == 0), 1.0, mult)
    b = mult * gated

    a = a.reshape(ng, V7X_SUBLANES, D_RNN)
    b = b.reshape(ng, V7X_SUBLANES, D_RNN)
    sub = lax.broadcasted_iota(jnp.int32, (ng, V7X_SUBLANES, D_RNN), 1)
    s = 1
    while s < V7X_SUBLANES:
        keep = sub >= s
        a_prev = jnp.where(keep, pltpu.roll(a, s, 1), 1.0)
        b_prev = jnp.where(keep, pltpu.roll(b, s, 1), 0.0)
        b = a * b_prev + b
        a = a * a_prev
        s *= 2
    h_in = h_ref[...]
    for g in range(ng):
        h = a[g] * h_in + b[g]
        hs_ref[g * V7X_SUBLANES:(g + 1) * V7X_SUBLANES, :] = h
        h_in = h[V7X_SUBLANES - 1:, :]
    h_ref[...] = h_in
    ht_ref[0] = h_in
    gr = _dot(u, w_ref[:, D_RNN:2 * D_RNN])
    ya_ref[...] = (hs_ref[...] * _gelu_tanh(gr)).astype(BF16)

    gu = _dot(u, w_ref[:, 2 * D_RNN:2 * D_RNN + D_GMLP])
    gv = _dot(u, w_ref[:, 2 * D_RNN + D_GMLP:BRANCH_COLS])
    vn = _layer_norm(gv, glng_ref[...], glnb_ref[...]).astype(BF16)
    ti = lax.broadcasted_iota(jnp.int32, (CHUNK, CHUNK), 0)
    si = lax.broadcasted_iota(jnp.int32, (CHUNK, CHUNK), 1)
    causal = si <= ti
    for g in range(GMLP_GROUPS):
        w = jnp.where(causal, ws_ref[g], 0.0).astype(BF16)
        c0 = g * GMLP_GW
        bias = bs_ref[:, c0:c0 + GMLP_GW]
        n_chunks = tt // CHUNK
        rhs = jnp.concatenate([vn[c * CHUNK:(c + 1) * CHUNK, c0:c0 + GMLP_GW]
                               for c in range(n_chunks)], axis=1)
        s_all = _dot(w, rhs)
        for c in range(n_chunks):
            r0 = c * CHUNK
            s_blk = s_all[:, c * GMLP_GW:(c + 1) * GMLP_GW] + bias
            yb_ref[r0:r0 + CHUNK, c0:c0 + GMLP_GW] = (
                gu[r0:r0 + CHUNK, c0:c0 + GMLP_GW] * s_blk).astype(BF16)


def _prompt_mixer(x, mod, w_branch, conv_w, conv_b, wbd, ba, bx, lam, gln_g, gln_b, ws, bs_tile,
                  *, batch, seq, tt):
    nt = seq // tt
    kern = functools.partial(_prompt_mixer_kernel, tt=tt)
    vec = pl.BlockSpec((1, D_RNN), lambda b, t: (0, 0))
    return pl.pallas_call(
        kern,
        grid=(batch, nt),
        in_specs=[
            pl.BlockSpec((tt, D_MODEL), lambda b, t: (b * nt + t, 0)),
            pl.BlockSpec((1, 3, 1, D_MODEL), lambda b, t: (b, 0, 0, 0)),
            pl.BlockSpec((D_MODEL, BRANCH_COLS), lambda b, t: (0, 0)),
            pl.BlockSpec((CONV_W, D_RNN), lambda b, t: (0, 0)),
            vec,
            pl.BlockSpec((D_RNN // V7X_MXU_DIM, V7X_MXU_DIM, 2 * V7X_MXU_DIM), lambda b, t: (0, 0, 0)),
            vec, vec, vec, vec, vec,
            pl.BlockSpec((GMLP_GROUPS, CHUNK, CHUNK), lambda b, t: (0, 0, 0)),
            pl.BlockSpec((CHUNK, D_GMLP), lambda b, t: (0, 0)),
        ],
        out_specs=[
            pl.BlockSpec((tt, D_RNN), lambda b, t: (b * nt + t, 0)),
            pl.BlockSpec((tt, D_GMLP), lambda b, t: (b * nt + t, 0)),
            pl.BlockSpec((1, 1, D_RNN), lambda b, t: (b, 0, 0)),
            pl.BlockSpec((1, V7X_SUBLANES, D_RNN), lambda b, t: (b, 0, 0)),
        ],
        out_shape=[
            jax.ShapeDtypeStruct((batch * seq, D_RNN), BF16),
            jax.ShapeDtypeStruct((batch * seq, D_GMLP), BF16),
            jax.ShapeDtypeStruct((batch, 1, D_RNN), F32),
            jax.ShapeDtypeStruct((batch, V7X_SUBLANES, D_RNN), F32),
        ],
        scratch_shapes=[pltpu.VMEM((V7X_SUBLANES, D_RNN), F32), pltpu.VMEM((1, D_RNN), F32),
                        pltpu.VMEM((tt, D_RNN), F32)],
        compiler_params=_params("arbitrary", "arbitrary"),
        name="prompt_mixer",
    )(x, mod, w_branch, conv_w, conv_b, wbd, ba, bx, lam, gln_g, gln_b, ws, bs_tile)


_PROJ_TN = 1024


def _sample_proj_kernel(x_ref, mod_ref, w_ref, lng_ref, lnb_ref, o_ref, wb_ref, u_ref, *, slabs):
    j = pl.program_id(0)

    @pl.when(j == 0)
    def _():
        u_ref[...] = _modulated_bf16(x_ref, mod_ref, slabs)

    wb_ref[...] = w_ref[...].astype(BF16)
    r = _dot(u_ref[...], wb_ref[...])

    @pl.when(jnp.logical_or(j == 0, j == 2))
    def _():
        o_ref[...] = r

    @pl.when(j == 1)
    def _():
        o_ref[...] = _gelu_tanh(r)

    @pl.when(j == 3)
    def _():
        o_ref[...] = _layer_norm(r, lng_ref[...], lnb_ref[...])


def _sample_proj(x, mod, w_in, gln_g, gln_b):
    rows = x.shape[0]
    mod_rows = mod.shape[2]
    kern = functools.partial(_sample_proj_kernel, slabs=_slabs(rows, mod_rows))
    w_spec = pl.BlockSpec((D_MODEL, _PROJ_TN), lambda j: (0, j))
    return pl.pallas_call(
        kern,
        grid=(BRANCH_COLS // _PROJ_TN,),
        in_specs=[
            pl.BlockSpec((rows, D_MODEL), lambda j: (0, 0)),
            pl.BlockSpec((1, 3, mod_rows, D_MODEL), lambda j: (0, 0, 0, 0)),
            w_spec,
            pl.BlockSpec((1, D_GMLP), lambda j: (0, 0)),
            pl.BlockSpec((1, D_GMLP), lambda j: (0, 0)),
        ],
        out_specs=[pl.BlockSpec((rows, _PROJ_TN), lambda j: (0, j)), w_spec],
        out_shape=[jax.ShapeDtypeStruct((rows, BRANCH_COLS), F32),
                   jax.ShapeDtypeStruct((D_MODEL, BRANCH_COLS), BF16)],
        scratch_shapes=[pltpu.VMEM((rows, D_MODEL), BF16)],
        compiler_params=_params("arbitrary"),
        name="sample_proj",
    )(x, mod, w_in, gln_g, gln_b)


def _sample_mix_kernel(xr_ref, gg_ref, gu_ref, vn_ref, sconv_ref, h0_ref, cw_ref, cb_ref,
                       wbd_ref, ba_ref, bx_ref, lam_ref, wsx_ref, bsx_ref,
                       ya_ref, yb_ref, ht_ref, xc_ref, *, nb, nt):
    def slab(t):
        return pl.ds(t * nb, nb)

    xpad = [sconv_ref[k] for k in range(CONV_W - 1)] + [xr_ref[slab(t), :] for t in range(nt)]
    for t in range(nt):
        xc = cb_ref[...]
        for k in range(CONV_W):
            xc = xc + xpad[t + k] * cw_ref[k:k + 1, :]
        xc_ref[slab(t), :] = xc

    a, mult, gated = _lru_coeffs(xc_ref[...], wbd_ref, ba_ref[...], bx_ref[...], lam_ref[...])
    b = mult * gated
    h = h0_ref[...]
    for t in range(nt):
        lo, hi = t * nb, (t + 1) * nb
        h = a[lo:hi] * h + b[lo:hi]
        ya_ref[slab(t), :] = (h * gg_ref[slab(t), :]).astype(BF16)
    ht_ref[...] = h

    for t in range(nt):
        s = bsx_ref[t:t + 1, :]
        for sp in range(t + 1):
            s = s + wsx_ref[t * nt + sp:t * nt + sp + 1, :] * vn_ref[slab(sp), :]
        yb_ref[slab(t), :] = (gu_ref[slab(t), :] * s).astype(BF16)


def _sample_mix(proj, sconv_tm, h0, conv_w, conv_b, wbd, ba, bx, lam, wsx, bsx, *, nb, nt):
    rows = nb * nt
    kern = functools.partial(_sample_mix_kernel, nb=nb, nt=nt)
    vec = pl.BlockSpec((1, D_RNN), lambda i: (0, 0))

    def col(jb):
        return pl.BlockSpec((rows, D_RNN), lambda i: (0, jb))

    return pl.pallas_call(
        kern,
        grid=(1,),
        in_specs=[
            col(0), col(1), col(2), col(3),
            pl.BlockSpec((CONV_W - 1, nb, D_RNN), lambda i: (0, 0, 0)),
            pl.BlockSpec((nb, D_RNN), lambda i: (0, 0)),
            pl.BlockSpec((CONV_W, D_RNN), lambda i: (0, 0)),
            vec,
            pl.BlockSpec((D_RNN // V7X_MXU_DIM, V7X_MXU_DIM, 2 * V7X_MXU_DIM), lambda i: (0, 0, 0)),
            vec, vec, vec,
            pl.BlockSpec((nt * nt, D_GMLP), lambda i: (0, 0)),
            pl.BlockSpec((nt, D_GMLP), lambda i: (0, 0)),
        ],
        out_specs=[
            pl.BlockSpec((rows, D_RNN), lambda i: (0, 0)),
            pl.BlockSpec((rows, D_GMLP), lambda i: (0, 0)),
            pl.BlockSpec((nb, D_RNN), lambda i: (0, 0)),
        ],
        out_shape=[
            jax.ShapeDtypeStruct((rows, D_RNN), BF16),
            jax.ShapeDtypeStruct((rows, D_GMLP), BF16),
            jax.ShapeDtypeStruct((nb, D_RNN), F32),
        ],
        scratch_shapes=[pltpu.VMEM((rows, D_RNN), F32)],
        compiler_params=_params("arbitrary"),
        name="sample_mix",
    )(proj, proj, proj, proj, sconv_tm, h0, conv_w, conv_b, wbd, ba, bx, lam, wsx, bsx)


def _merge_kernel(ya_ref, yb_ref, x_ref, mod_ref, wgate_ref, wpa_ref, wpb_ref, wo_ref,
                  lng_ref, lnb_ref, *rest, slabs):
    if len(rest) == 1:
        (o_ref,) = rest
    else:
        wgu_ref, wdn_ref, o_ref, wgu_b_ref, wdn_b_ref = rest
        wgu_b_ref[...] = wgu_ref[...].astype(BF16)
        wdn_b_ref[...] = wdn_ref[...].astype(BF16)
    u = _modulated_bf16(x_ref, mod_ref, slabs)
    y_a = jax.nn.sigmoid(_dot(u, wgate_ref[:, :D_MODEL])) * _dot(ya_ref[...], wpa_ref[...])
    y_b = jax.nn.sigmoid(_dot(u, wgate_ref[:, D_MODEL:])) * _dot(yb_ref[...], wpb_ref[...])
    mix = _dot((y_a + y_b).astype(BF16), wo_ref[...])
    for r0, nr in slabs:
        y = ALPHA * x_ref[r0:r0 + nr, :] + mod_ref[0, 2] * mix[r0:r0 + nr, :]
        o_ref[r0:r0 + nr, :] = _layer_norm(y, lng_ref[...], lnb_ref[...])


def _merge(ya, yb, x, mod, w_gate, w_pa, w_pb, w_out, ln_g, ln_b, *, tm, rows_per_mod,
           cast_weights=None):
    rows = x.shape[0]
    mod_rows = mod.shape[2]
    steps = rows // tm
    kern = functools.partial(_merge_kernel, slabs=_slabs(tm, mod_rows))
    tiles_per_mod = rows_per_mod // tm
    whole = lambda shape: pl.BlockSpec(shape, lambda i: (0, 0))
    in_specs = [
        pl.BlockSpec((tm, D_RNN), lambda i: (i, 0)),
        pl.BlockSpec((tm, D_GMLP), lambda i: (i, 0)),
        pl.BlockSpec((tm, D_MODEL), lambda i: (i, 0)),
        pl.BlockSpec((1, 3, mod_rows, D_MODEL), lambda i: (i // tiles_per_mod, 0, 0, 0)),
        whole((D_MODEL, 2 * D_MODEL)),
        whole((D_RNN, D_MODEL)),
        whole((D_GMLP, D_MODEL)),
        whole((D_MODEL, D_MODEL)),
        whole((1, D_MODEL)),
        whole((1, D_MODEL)),
    ]
    args = [ya, yb, x, mod, w_gate, w_pa, w_pb, w_out, ln_g, ln_b]
    out_specs = [pl.BlockSpec((tm, D_MODEL), lambda i: (i, 0))]
    out_shape = [jax.ShapeDtypeStruct((rows, D_MODEL), F32)]
    if cast_weights is not None:
        for w in cast_weights:
            slab = pl.BlockSpec((w.shape[0] // steps, w.shape[1]), lambda i: (i, 0))
            in_specs.append(slab)
            args.append(w)
            out_specs.append(slab)
            out_shape.append(jax.ShapeDtypeStruct(w.shape, BF16))
    return pl.pallas_call(
        kern,
        grid=(steps,),
        in_specs=in_specs,
        out_specs=out_specs,
        out_shape=out_shape,
        compiler_params=_params("arbitrary", vmem_limit_bytes=V7X_VMEM_FFN_LIMIT_BYTES),
        name="merge_out",
    )(*args)


def _block_diag_tiles(w):
    per = V7X_MXU_DIM // RNN_BW
    w4 = w.reshape(RNN_BLOCKS // per, per, RNN_BW, RNN_BW)
    eye = jnp.eye(per, dtype=w.dtype)
    t = w4[:, :, :, None, :] * eye[None, :, None, :, None]
    return t.reshape(RNN_BLOCKS // per, V7X_MXU_DIM, V7X_MXU_DIM)


def kernel(x_prompt, x_sample, state_conv, state_h, c_prompt, c_sample, w_ada, b_ada, ffn1_w_gu, ffn1_w_down, ffn2_w_gu, ffn2_w_down, w_in, conv_w, conv_b, lru_wa, lru_ba, lru_wx, lru_bx, lru_lambda, gmlp_ln_g, gmlp_ln_b, gmlp_ws, gmlp_bs, w_pa, w_pb, w_out, ln_g, ln_b):
    assert w_ada.shape[0] == DEPTH == 1
    bp, seq, _ = x_prompt.shape
    bs, dseq, _ = x_sample.shape
    l = 0

    w_gate = w_in[l, :, BRANCH_COLS:].astype(BF16)
    w_pa_b, w_pb_b, w_out_b = w_pa[l].astype(BF16), w_pb[l].astype(BF16), w_out[l].astype(BF16)
    wbd = jnp.concatenate([_block_diag_tiles(lru_wa[l]), _block_diag_tiles(lru_wx[l])],
                          axis=-1).astype(BF16)

    row = lambda v: v.reshape(1, -1)
    cb, ba, bx, lam = row(conv_b[l]), row(lru_ba[l]), row(lru_bx[l]), row(lru_lambda[l])
    gln_g, gln_b = row(gmlp_ln_g[l]), row(gmlp_ln_b[l])
    lng = [row(ln_g[l, k]) for k in range(3)]
    lnb = [row(ln_b[l, k]) for k in range(3)]

    bs_tile = jnp.repeat(gmlp_bs[l].T, GMLP_GW, axis=1)
    wsx = jnp.repeat(jnp.transpose(gmlp_ws[l][:, :dseq, :dseq], (1, 2, 0)), GMLP_GW, axis=2)
    wsx = wsx.reshape(dseq * dseq, D_GMLP)
    bsx = bs_tile[:dseq]

    c_all = jnp.concatenate([c_prompt, c_sample], axis=0)
    mod = _modulation(c_all, w_ada[l], row(b_ada[l]))
    mod_p = mod[:bp].reshape(bp, N_MOD, 1, D_MODEL)
    mod_s = jnp.transpose(mod[bp:].reshape(bs, N_MOD, D_MODEL), (1, 0, 2))[None]

    xp = x_prompt.reshape(bp * seq, D_MODEL)
    xs = jnp.transpose(x_sample, (1, 0, 2)).reshape(dseq * bs, D_MODEL)
    n_s = dseq * bs
    keep = CONV_W - 1

    def ffn(x, modx, k, w_g, w_v, v_col0, w_down, tm, tf, rows_per_mod, emit_bf16=False):
        return _ffn(x, modx[:, 3 * k:3 * k + 3], w_g, w_v, v_col0, w_down, lng[k], lnb[k],
                    tm=tm, tf=tf, rows_per_mod=rows_per_mod, emit_bf16=emit_bf16)

    def merge(ya, yb, x1, modx, rows_per_mod, cast_weights=None):
        return _merge(ya, yb, x1, modx[:, 3:6], w_gate, w_pa_b, w_pb_b, w_out_b, lng[1], lnb[1],
                      tm=256, rows_per_mod=rows_per_mod, cast_weights=cast_weights)

    x1p, w_g_b, w_v_b, w_d_b = ffn(xp, mod_p, 0, ffn1_w_gu[l], ffn1_w_gu[l], D_FF, ffn1_w_down[l],
                                   1024, 256, seq, emit_bf16=True)
    (x1s,) = ffn(xs, mod_s, 0, w_g_b, w_v_b, 0, w_d_b, n_s, 512, n_s)

    proj_s, w_branch = _sample_proj(x1s, mod_s[:, 3:6], w_in[l], gln_g, gln_b)
    sconv_tm = jnp.transpose(state_conv[l], (1, 0, 2))
    ya_s, yb_s, ht_s = _sample_mix(proj_s, sconv_tm, state_h[l], conv_w[l], cb, wbd, ba, bx, lam,
                                   wsx, bsx, nb=bs, nt=dseq)
    ya_p, yb_p, ht_p, tail_p = _prompt_mixer(
        x1p, mod_p[:, 3:6], w_branch, conv_w[l], cb, wbd, ba, bx, lam, gln_g, gln_b,
        gmlp_ws[l], bs_tile, batch=bp, seq=seq, tt=512)
    (x2s,) = merge(ya_s, yb_s, x1s, mod_s, n_s)
    x2p, w_gu2_b, w_dn2_b = merge(ya_p, yb_p, x1p, mod_p, seq,
                                  cast_weights=(ffn2_w_gu[l], ffn2_w_down[l]))

    (x3p,) = ffn(x2p, mod_p, 2, w_gu2_b, w_gu2_b, D_FF, w_dn2_b, 1024, 512, seq)
    (x3s,) = ffn(x2s, mod_s, 2, w_gu2_b, w_gu2_b, D_FF, w_dn2_b, n_s, 512, n_s)

    y_prompt = x3p.reshape(bp, seq, D_MODEL)
    y_sample = jnp.transpose(x3s.reshape(dseq, bs, D_MODEL), (1, 0, 2))
    new_conv_p = tail_p[:, V7X_SUBLANES - keep:][None]
    new_h_p = ht_p.reshape(bp, D_RNN)[None]
    xr_s = lax.slice(proj_s.reshape(dseq, bs, BRANCH_COLS), (dseq - keep, 0, 0), (dseq, bs, D_RNN))
    new_conv_s = jnp.transpose(xr_s, (1, 0, 2))[None]
    new_h_s = ht_s[None]
    vn_s = lax.slice(proj_s.reshape(dseq, bs, BRANCH_COLS), (0, 0, 2 * D_RNN + D_GMLP),
                     (dseq, bs, BRANCH_COLS))
    new_v_s = jnp.transpose(vn_s, (1, 0, 2))[None]
    return (y_prompt, y_sample, new_conv_p, new_h_p, new_conv_s, new_h_s, new_v_s)
```

```python
import functools
import math

import jax
import jax.numpy as jnp
from jax import lax
from jax.experimental import pallas as pl
from jax.experimental.pallas import tpu as pltpu

D_MODEL = 2048
D_RNN = 1024
RNN_BLOCKS = 16
RNN_BW = D_RNN // RNN_BLOCKS
CONV_W = 4
LRU_C = 8.0
D_GMLP = 1024
CHUNK = 128
GMLP_GROUPS = 8
GMLP_GW = D_GMLP // GMLP_GROUPS
D_FF = 5632
N_MOD = 9
IN_COLS = 2 * D_RNN + 2 * D_GMLP + 2 * D_MODEL
BRANCH_COLS = 2 * D_RNN + 2 * D_GMLP
LN_EPS = 1e-5
DEPTH = 1
ALPHA = (2.0 * DEPTH) ** 0.25

V7X_SUBLANES = 8
V7X_LANES = 128
V7X_MXU_DIM = 256
V7X_VMEM_BYTES = 64 * 1024 * 1024
V7X_VMEM_LIMIT_BYTES = 58 * 1024 * 1024
V7X_VMEM_FFN_LIMIT_BYTES = V7X_VMEM_BYTES - 512 * 1024

BF16 = jnp.bfloat16
F32 = jnp.float32


def _dot(a, b):
    return jnp.dot(a, b, preferred_element_type=F32)


def _layer_norm(x, g, b):
    mu = jnp.mean(x, axis=-1, keepdims=True)
    xc = x - mu
    var = jnp.mean(xc * xc, axis=-1, keepdims=True)
    return xc * lax.rsqrt(var + LN_EPS) * g + b


def _gelu_tanh(x):
    c = math.sqrt(2.0 / math.pi)
    return 0.5 * x * (1.0 + jnp.tanh(c * (x + 0.044715 * (x * x * x))))


def _softplus(x):
    return jnp.maximum(x, 0.0) + jnp.log1p(jnp.exp(-jnp.abs(x)))


def _params(*semantics, vmem_limit_bytes=V7X_VMEM_LIMIT_BYTES):
    return pltpu.CompilerParams(dimension_semantics=semantics, vmem_limit_bytes=vmem_limit_bytes)


def _slabs(tm, mod_rows):
    if mod_rows == 1:
        return [(0, tm)]
    return [(s, mod_rows) for s in range(0, tm, mod_rows)]


def _modulated_bf16(x_ref, mod_ref, slabs):
    parts = [(x_ref[r0:r0 + nr, :] * (1.0 + mod_ref[0, 1]) + mod_ref[0, 0]).astype(BF16)
             for r0, nr in slabs]
    return parts[0] if len(parts) == 1 else jnp.concatenate(parts, axis=0)


def _mod_kernel(c_ref, w_ref, b_ref, o_ref):
    c = c_ref[...]
    a = (c * jax.nn.sigmoid(c)).astype(BF16)
    o_ref[...] = _dot(a, w_ref[...].astype(BF16)) + b_ref[...]


def _modulation(c_all, w_ada, b_ada, tn=2048):
    rows = c_all.shape[0]
    n = w_ada.shape[1]
    return pl.pallas_call(
        _mod_kernel,
        grid=(n // tn,),
        in_specs=[
            pl.BlockSpec((rows, D_MODEL), lambda j: (0, 0)),
            pl.BlockSpec((D_MODEL, tn), lambda j: (0, j)),
            pl.BlockSpec((1, tn), lambda j: (0, j)),
        ],
        out_specs=pl.BlockSpec((rows, tn), lambda j: (0, j)),
        out_shape=jax.ShapeDtypeStruct((rows, n), F32),
        compiler_params=_params("arbitrary"),
        name="adaln_mod",
    )(c_all, w_ada, b_ada)


def _ffn_kernel(x_ref, mod_ref, wg_ref, wv_ref, wd_ref, lng_ref, lnb_ref, o_ref, *rest,
                slabs, n_f, emit_bf16):
    u_ref = rest[-1]
    j = pl.program_id(1)

    def swiglu_down():
        if emit_bf16:
            wg, wv, wd = (r[...].astype(BF16) for r in (wg_ref, wv_ref, wd_ref))
            for dst, w in zip(rest[:3], (wg, wv, wd)):
                dst[...] = w
        else:
            wg, wv, wd = wg_ref[...], wv_ref[...], wd_ref[...]
        u = u_ref[...]
        g = _dot(u, wg)
        v = _dot(u, wv)
        h = (g * jax.nn.sigmoid(g) * v).astype(BF16)
        return _dot(h, wd)

    @pl.when(j == 0)
    def _():
        u_ref[...] = _modulated_bf16(x_ref, mod_ref, slabs)
        o_ref[...] = swiglu_down()

    @pl.when(jnp.logical_and(j > 0, j < n_f - 1))
    def _():
        o_ref[...] += swiglu_down()

    @pl.when(j == n_f - 1)
    def _():
        o_ref[...] += swiglu_down()
        for r0, nr in slabs:
            rows = pl.ds(r0, nr)
            y = ALPHA * x_ref[rows, :] + 0.5 * mod_ref[0, 2] * o_ref[rows, :]
            o_ref[rows, :] = _layer_norm(y, lng_ref[...], lnb_ref[...])


def _ffn(x, mod, w_g, w_v, v_col0, w_down, ln_g, ln_b, *, tm, tf, rows_per_mod, emit_bf16):
    rows = x.shape[0]
    mod_rows = mod.shape[2]
    n_f = D_FF // tf
    v_blk0 = v_col0 // tf
    kern = functools.partial(_ffn_kernel, slabs=_slabs(tm, mod_rows), n_f=n_f,
                             emit_bf16=emit_bf16)
    tiles_per_mod = rows_per_mod // tm
    up_spec = pl.BlockSpec((D_MODEL, tf), lambda i, j: (0, j))
    down_spec = pl.BlockSpec((tf, D_MODEL), lambda i, j: (j, 0))
    out_specs = [pl.BlockSpec((tm, D_MODEL), lambda i, j: (i, 0))]
    out_shape = [jax.ShapeDtypeStruct((rows, D_MODEL), F32)]
    if emit_bf16:
        pad = tf if rows > tm else 0
        emit_blk = lambda i, j: jnp.where(i == 0, j, n_f)
        out_specs += [pl.BlockSpec((D_MODEL, tf), lambda i, j: (0, emit_blk(i, j)))] * 2
        out_specs += [pl.BlockSpec((tf, D_MODEL), lambda i, j: (emit_blk(i, j), 0))]
        out_shape += [jax.ShapeDtypeStruct((D_MODEL, D_FF + pad), BF16)] * 2
        out_shape += [jax.ShapeDtypeStruct((D_FF + pad, D_MODEL), BF16)]
    return pl.pallas_call(
        kern,
        grid=(rows // tm, n_f),
        in_specs=[
            pl.BlockSpec((tm, D_MODEL), lambda i, j: (i, 0)),
            pl.BlockSpec((1, 3, mod_rows, D_MODEL), lambda i, j: (i // tiles_per_mod, 0, 0, 0)),
            up_spec,
            pl.BlockSpec((D_MODEL, tf), lambda i, j: (0, v_blk0 + j)),
            down_spec,
            pl.BlockSpec((1, D_MODEL), lambda i, j: (0, 0)),
            pl.BlockSpec((1, D_MODEL), lambda i, j: (0, 0)),
        ],
        out_specs=out_specs,
        out_shape=out_shape,
        scratch_shapes=[pltpu.VMEM((tm, D_MODEL), BF16)],
        compiler_params=_params("arbitrary", "arbitrary", vmem_limit_bytes=V7X_VMEM_FFN_LIMIT_BYTES),
        name="macaron_ffn",
    )(x, mod, w_g, w_v, w_down, ln_g, ln_b)


def _lru_coeffs(xc, wbd_ref, ba, bx, lam):
    xcb = xc.astype(BF16)
    ra, rx = [], []
    for q in range(D_RNN // V7X_MXU_DIM):
        cols = slice(q * V7X_MXU_DIM, (q + 1) * V7X_MXU_DIM)
        rq = _dot(xcb[:, cols], wbd_ref[q])
        ra.append(rq[:, :V7X_MXU_DIM])
        rx.append(rq[:, V7X_MXU_DIM:])
    r = jax.nn.sigmoid(jnp.concatenate(ra, axis=-1) + ba)
    i = jax.nn.sigmoid(jnp.concatenate(rx, axis=-1) + bx)
    log_a = (-LRU_C * r) * _softplus(-lam)
    a = jnp.exp(log_a)
    z = -jnp.tanh(log_a) * (a * a + 1.0)
    mult = jnp.where(z > 0.0, z * lax.rsqrt(z), 0.0)
    return a, mult, i * xc


_MIXER_INPUTS = 13


def _prompt_mixer_kernel(*refs, tt, n_cast):
    (x_ref, mod_ref, w_ref, cw_ref, cb_ref, wbd_ref, ba_ref, bx_ref, lam_ref,
     glng_ref, glnb_ref, ws_ref, bs_ref) = refs[:_MIXER_INPUTS]
    cast_in = refs[_MIXER_INPUTS:_MIXER_INPUTS + n_cast]
    ya_ref, yb_ref, ht_ref, tailo_ref = refs[_MIXER_INPUTS + n_cast:_MIXER_INPUTS + n_cast + 4]
    cast_out = refs[_MIXER_INPUTS + n_cast + 4:_MIXER_INPUTS + 2 * n_cast + 4]
    tail_ref, h_ref, hs_ref = refs[_MIXER_INPUTS + 2 * n_cast + 4:]
    t = pl.program_id(1)
    ng = tt // V7X_SUBLANES

    for src, dst in zip(cast_in, cast_out):
        dst[...] = src[...].astype(BF16)

    @pl.when(t == 0)
    def _():
        tail_ref[...] = jnp.zeros_like(tail_ref)
        h_ref[...] = jnp.zeros_like(h_ref)

    u = _modulated_bf16(x_ref, mod_ref, [(0, tt)])

    xr = _dot(u, w_ref[:, 0:D_RNN])
    prev = tail_ref[...]
    row = lax.broadcasted_iota(jnp.int32, (tt, D_RNN), 0)
    row8 = lax.broadcasted_iota(jnp.int32, (V7X_SUBLANES, D_RNN), 0)
    xc = cb_ref[...]
    for k in range(CONV_W):
        s = CONV_W - 1 - k
        if s == 0:
            xs = xr
        else:
            rolled = pltpu.roll(xr, s, 0)
            head = jnp.where(row8 < s, pltpu.roll(prev, s, 0), rolled[:V7X_SUBLANES])
            xs = jnp.concatenate([head, rolled[V7X_SUBLANES:]], axis=0)
        xc = xc + xs * cw_ref[k:k + 1, :]
    new_tail = xr[tt - V7X_SUBLANES:, :]
    tail_ref[...] = new_tail
    tailo_ref[0] = new_tail

    a, mult, gated = _lru_coeffs(xc, wbd_ref, ba_ref[...], bx_ref[...], lam_ref[...])
    mult = jnp.where(jnp.logical_and(row == 0, t == 0), 1.0, mult)
    b = mult * gated

    a = a.reshape(ng, V7X_SUBLANES, D_RNN)
    b = b.reshape(ng, V7X_SUBLANES, D_RNN)
    sub = lax.broadcasted_iota(jnp.int32, (ng, V7X_SUBLANES, D_RNN), 1)
    s = 1
    while s < V7X_SUBLANES:
        keep = sub >= s
        a_prev = jnp.where(keep, pltpu.roll(a, s, 1), 1.0)
        b_prev = jnp.where(keep, pltpu.roll(b, s, 1), 0.0)
        b = a * b_prev + b
        a = a * a_prev
        s *= 2
    h_in = h_ref[...]
    for g in range(ng):
        h = a[g] * h_in + b[g]
        hs_ref[g * V7X_SUBLANES:(g + 1) * V7X_SUBLANES, :] = h
        h_in = h[V7X_SUBLANES - 1:, :]
    h_ref[...] = h_in
    ht_ref[0] = h_in
    gr = _dot(u, w_ref[:, D_RNN:2 * D_RNN])
    ya_ref[...] = (hs_ref[...] * _gelu_tanh(gr)).astype(BF16)

    gu = _dot(u, w_ref[:, 2 * D_RNN:2 * D_RNN + D_GMLP])
    gv = _dot(u, w_ref[:, 2 * D_RNN + D_GMLP:BRANCH_COLS])
    vn = _layer_norm(gv, glng_ref[...], glnb_ref[...]).astype(BF16)
    ti = lax.broadcasted_iota(jnp.int32, (CHUNK, CHUNK), 0)
    si = lax.broadcasted_iota(jnp.int32, (CHUNK, CHUNK), 1)
    causal = si <= ti
    for g in range(GMLP_GROUPS):
        w = jnp.where(causal, ws_ref[g], 0.0).astype(BF16)
        c0 = g * GMLP_GW
        bias = bs_ref[:, c0:c0 + GMLP_GW]
        n_chunks = tt // CHUNK
        rhs = jnp.concatenate([vn[c * CHUNK:(c + 1) * CHUNK, c0:c0 + GMLP_GW]
                               for c in range(n_chunks)], axis=1)
        s_all = _dot(w, rhs)
        for c in range(n_chunks):
            r0 = c * CHUNK
            s_blk = s_all[:, c * GMLP_GW:(c + 1) * GMLP_GW] + bias
            yb_ref[r0:r0 + CHUNK, c0:c0 + GMLP_GW] = (
                gu[r0:r0 + CHUNK, c0:c0 + GMLP_GW] * s_blk).astype(BF16)


def _prompt_mixer(x, mod, w_branch, conv_w, conv_b, wbd, ba, bx, lam, gln_g, gln_b, ws, bs_tile,
                  *, batch, seq, tt, cast_weights=()):
    nt = seq // tt
    steps = batch * nt
    kern = functools.partial(_prompt_mixer_kernel, tt=tt, n_cast=len(cast_weights))
    vec = pl.BlockSpec((1, D_RNN), lambda b, t: (0, 0))
    cast_in_specs = [pl.BlockSpec((w.shape[0] // steps, nc), lambda b, t, cb=cb: (b * nt + t, cb))
                     for w, nc, cb in cast_weights]
    cast_out_specs = [pl.BlockSpec((w.shape[0] // steps, nc), lambda b, t: (b * nt + t, 0))
                      for w, nc, cb in cast_weights]
    return pl.pallas_call(
        kern,
        grid=(batch, nt),
        in_specs=[
            pl.BlockSpec((tt, D_MODEL), lambda b, t: (b * nt + t, 0)),
            pl.BlockSpec((1, 3, 1, D_MODEL), lambda b, t: (b, 0, 0, 0)),
            pl.BlockSpec((D_MODEL, BRANCH_COLS), lambda b, t: (0, 0)),
            pl.BlockSpec((CONV_W, D_RNN), lambda b, t: (0, 0)),
            vec,
            pl.BlockSpec((D_RNN // V7X_MXU_DIM, V7X_MXU_DIM, 2 * V7X_MXU_DIM), lambda b, t: (0, 0, 0)),
            vec, vec, vec, vec, vec,
            pl.BlockSpec((GMLP_GROUPS, CHUNK, CHUNK), lambda b, t: (0, 0, 0)),
            pl.BlockSpec((CHUNK, D_GMLP), lambda b, t: (0, 0)),
        ] + cast_in_specs,
        out_specs=[
            pl.BlockSpec((tt, D_RNN), lambda b, t: (b * nt + t, 0)),
            pl.BlockSpec((tt, D_GMLP), lambda b, t: (b * nt + t, 0)),
            pl.BlockSpec((1, 1, D_RNN), lambda b, t: (b, 0, 0)),
            pl.BlockSpec((1, V7X_SUBLANES, D_RNN), lambda b, t: (b, 0, 0)),
        ] + cast_out_specs,
        out_shape=[
            jax.ShapeDtypeStruct((batch * seq, D_RNN), BF16),
            jax.ShapeDtypeStruct((batch * seq, D_GMLP), BF16),
            jax.ShapeDtypeStruct((batch, 1, D_RNN), F32),
            jax.ShapeDtypeStruct((batch, V7X_SUBLANES, D_RNN), F32),
        ] + [jax.ShapeDtypeStruct((w.shape[0], nc), BF16) for w, nc, cb in cast_weights],
        scratch_shapes=[pltpu.VMEM((V7X_SUBLANES, D_RNN), F32), pltpu.VMEM((1, D_RNN), F32),
                        pltpu.VMEM((tt, D_RNN), F32)],
        compiler_params=_params("arbitrary", "arbitrary"),
        name="prompt_mixer",
    )(x, mod, w_branch, conv_w, conv_b, wbd, ba, bx, lam, gln_g, gln_b, ws, bs_tile,
      *[w for w, nc, cb in cast_weights])


_PROJ_TN = 1024


def _sample_proj_kernel(x_ref, mod_ref, w_ref, lng_ref, lnb_ref, o_ref, wb_ref, u_ref, *, slabs):
    j = pl.program_id(0)

    @pl.when(j == 0)
    def _():
        u_ref[...] = _modulated_bf16(x_ref, mod_ref, slabs)

    wb_ref[...] = w_ref[...].astype(BF16)
    r = _dot(u_ref[...], wb_ref[...])

    @pl.when(jnp.logical_or(j == 0, j == 2))
    def _():
        o_ref[...] = r

    @pl.when(j == 1)
    def _():
        o_ref[...] = _gelu_tanh(r)

    @pl.when(j == 3)
    def _():
        o_ref[...] = _layer_norm(r, lng_ref[...], lnb_ref[...])


def _sample_proj(x, mod, w_in, gln_g, gln_b):
    rows = x.shape[0]
    mod_rows = mod.shape[2]
    kern = functools.partial(_sample_proj_kernel, slabs=_slabs(rows, mod_rows))
    w_spec = pl.BlockSpec((D_MODEL, _PROJ_TN), lambda j: (0, j))
    return pl.pallas_call(
        kern,
        grid=(BRANCH_COLS // _PROJ_TN,),
        in_specs=[
            pl.BlockSpec((rows, D_MODEL), lambda j: (0, 0)),
            pl.BlockSpec((1, 3, mod_rows, D_MODEL), lambda j: (0, 0, 0, 0)),
            w_spec,
            pl.BlockSpec((1, D_GMLP), lambda j: (0, 0)),
            pl.BlockSpec((1, D_GMLP), lambda j: (0, 0)),
        ],
        out_specs=[pl.BlockSpec((rows, _PROJ_TN), lambda j: (0, j)), w_spec],
        out_shape=[jax.ShapeDtypeStruct((rows, BRANCH_COLS), F32),
                   jax.ShapeDtypeStruct((D_MODEL, BRANCH_COLS), BF16)],
        scratch_shapes=[pltpu.VMEM((rows, D_MODEL), BF16)],
        compiler_params=_params("arbitrary"),
        name="sample_proj",
    )(x, mod, w_in, gln_g, gln_b)


def _sample_mix_kernel(xr_ref, gg_ref, gu_ref, vn_ref, sconv_ref, h0_ref, cw_ref, cb_ref,
                       wbd_ref, ba_ref, bx_ref, lam_ref, wsx_ref, bsx_ref,
                       ya_ref, yb_ref, ht_ref, xc_ref, *, nb, nt):
    def slab(t):
        return pl.ds(t * nb, nb)

    xpad = [sconv_ref[k] for k in range(CONV_W - 1)] + [xr_ref[slab(t), :] for t in range(nt)]
    for t in range(nt):
        xc = cb_ref[...]
        for k in range(CONV_W):
            xc = xc + xpad[t + k] * cw_ref[k:k + 1, :]
        xc_ref[slab(t), :] = xc

    a, mult, gated = _lru_coeffs(xc_ref[...], wbd_ref, ba_ref[...], bx_ref[...], lam_ref[...])
    b = mult * gated
    h = h0_ref[...]
    for t in range(nt):
        lo, hi = t * nb, (t + 1) * nb
        h = a[lo:hi] * h + b[lo:hi]
        ya_ref[slab(t), :] = (h * gg_ref[slab(t), :]).astype(BF16)
    ht_ref[...] = h

    for t in range(nt):
        s = bsx_ref[t:t + 1, :]
        for sp in range(t + 1):
            s = s + wsx_ref[t * nt + sp:t * nt + sp + 1, :] * vn_ref[slab(sp), :]
        yb_ref[slab(t), :] = (gu_ref[slab(t), :] * s).astype(BF16)


def _sample_mix(proj, sconv_tm, h0, conv_w, conv_b, wbd, ba, bx, lam, wsx, bsx, *, nb, nt):
    rows = nb * nt
    kern = functools.partial(_sample_mix_kernel, nb=nb, nt=nt)
    vec = pl.BlockSpec((1, D_RNN), lambda i: (0, 0))

    def col(jb):
        return pl.BlockSpec((rows, D_RNN), lambda i: (0, jb))

    return pl.pallas_call(
        kern,
        grid=(1,),
        in_specs=[
            col(0), col(1), col(2), col(3),
            pl.BlockSpec((CONV_W - 1, nb, D_RNN), lambda i: (0, 0, 0)),
            pl.BlockSpec((nb, D_RNN), lambda i: (0, 0)),
            pl.BlockSpec((CONV_W, D_RNN), lambda i: (0, 0)),
            vec,
            pl.BlockSpec((D_RNN // V7X_MXU_DIM, V7X_MXU_DIM, 2 * V7X_MXU_DIM), lambda i: (0, 0, 0)),
            vec, vec, vec,
            pl.BlockSpec((nt * nt, D_GMLP), lambda i: (0, 0)),
            pl.BlockSpec((nt, D_GMLP), lambda i: (0, 0)),
        ],
        out_specs=[
            pl.BlockSpec((rows, D_RNN), lambda i: (0, 0)),
            pl.BlockSpec((rows, D_GMLP), lambda i: (0, 0)),
            pl.BlockSpec((nb, D_RNN), lambda i: (0, 0)),
        ],
        out_shape=[
            jax.ShapeDtypeStruct((rows, D_RNN), BF16),
            jax.ShapeDtypeStruct((rows, D_GMLP), BF16),
            jax.ShapeDtypeStruct((nb, D_RNN), F32),
        ],
        scratch_shapes=[pltpu.VMEM((rows, D_RNN), F32)],
        compiler_params=_params("arbitrary"),
        name="sample_mix",
    )(proj, proj, proj, proj, sconv_tm, h0, conv_w, conv_b, wbd, ba, bx, lam, wsx, bsx)


def _merge_kernel(ya_ref, yb_ref, x_ref, mod_ref, wgate_ref, wpa_ref, wpb_ref, wo_ref,
                  lng_ref, lnb_ref, *rest, slabs):
    if len(rest) == 1:
        (o_ref,) = rest
    else:
        wgu_ref, wdn_ref, o_ref, wgu_b_ref, wdn_b_ref = rest
        wgu_b_ref[...] = wgu_ref[...].astype(BF16)
        wdn_b_ref[...] = wdn_ref[...].astype(BF16)
    u = _modulated_bf16(x_ref, mod_ref, slabs)
    y_a = jax.nn.sigmoid(_dot(u, wgate_ref[:, :D_MODEL])) * _dot(ya_ref[...], wpa_ref[...])
    y_b = jax.nn.sigmoid(_dot(u, wgate_ref[:, D_MODEL:])) * _dot(yb_ref[...], wpb_ref[...])
    mix = _dot((y_a + y_b).astype(BF16), wo_ref[...])
    for r0, nr in slabs:
        y = ALPHA * x_ref[r0:r0 + nr, :] + mod_ref[0, 2] * mix[r0:r0 + nr, :]
        o_ref[r0:r0 + nr, :] = _layer_norm(y, lng_ref[...], lnb_ref[...])


def _merge(ya, yb, x, mod, w_gate, w_pa, w_pb, w_out, ln_g, ln_b, *, tm, rows_per_mod,
           cast_weights=None):
    rows = x.shape[0]
    mod_rows = mod.shape[2]
    steps = rows // tm
    kern = functools.partial(_merge_kernel, slabs=_slabs(tm, mod_rows))
    tiles_per_mod = rows_per_mod // tm
    whole = lambda shape: pl.BlockSpec(shape, lambda i: (0, 0))
    in_specs = [
        pl.BlockSpec((tm, D_RNN), lambda i: (i, 0)),
        pl.BlockSpec((tm, D_GMLP), lambda i: (i, 0)),
        pl.BlockSpec((tm, D_MODEL), lambda i: (i, 0)),
        pl.BlockSpec((1, 3, mod_rows, D_MODEL), lambda i: (i // tiles_per_mod, 0, 0, 0)),
        whole((D_MODEL, 2 * D_MODEL)),
        whole((D_RNN, D_MODEL)),
        whole((D_GMLP, D_MODEL)),
        whole((D_MODEL, D_MODEL)),
        whole((1, D_MODEL)),
        whole((1, D_MODEL)),
    ]
    args = [ya, yb, x, mod, w_gate, w_pa, w_pb, w_out, ln_g, ln_b]
    out_specs = [pl.BlockSpec((tm, D_MODEL), lambda i: (i, 0))]
    out_shape = [jax.ShapeDtypeStruct((rows, D_MODEL), F32)]
    if cast_weights is not None:
        for w in cast_weights:
            slab = pl.BlockSpec((w.shape[0] // steps, w.shape[1]), lambda i: (i, 0))
            in_specs.append(slab)
            args.append(w)
            out_specs.append(slab)
            out_shape.append(jax.ShapeDtypeStruct(w.shape, BF16))
    return pl.pallas_call(
        kern,
        grid=(steps,),
        in_specs=in_specs,
        out_specs=out_specs,
        out_shape=out_shape,
        compiler_params=_params("arbitrary", vmem_limit_bytes=V7X_VMEM_FFN_LIMIT_BYTES),
        name="merge_out",
    )(*args)


def _block_diag_tiles(w):
    per = V7X_MXU_DIM // RNN_BW
    w4 = w.reshape(RNN_BLOCKS // per, per, RNN_BW, RNN_BW)
    eye = jnp.eye(per, dtype=w.dtype)
    t = w4[:, :, :, None, :] * eye[None, :, None, :, None]
    return t.reshape(RNN_BLOCKS // per, V7X_MXU_DIM, V7X_MXU_DIM)


def kernel(x_prompt, x_sample, state_conv, state_h, c_prompt, c_sample, w_ada, b_ada, ffn1_w_gu, ffn1_w_down, ffn2_w_gu, ffn2_w_down, w_in, conv_w, conv_b, lru_wa, lru_ba, lru_wx, lru_bx, lru_lambda, gmlp_ln_g, gmlp_ln_b, gmlp_ws, gmlp_bs, w_pa, w_pb, w_out, ln_g, ln_b):
    assert w_ada.shape[0] == DEPTH == 1
    bp, seq, _ = x_prompt.shape
    bs, dseq, _ = x_sample.shape
    l = 0

    wbd = jnp.concatenate([_block_diag_tiles(lru_wa[l]), _block_diag_tiles(lru_wx[l])],
                          axis=-1).astype(BF16)

    row = lambda v: v.reshape(1, -1)
    cb, ba, bx, lam = row(conv_b[l]), row(lru_ba[l]), row(lru_bx[l]), row(lru_lambda[l])
    gln_g, gln_b = row(gmlp_ln_g[l]), row(gmlp_ln_b[l])
    lng = [row(ln_g[l, k]) for k in range(3)]
    lnb = [row(ln_b[l, k]) for k in range(3)]

    bs_tile = jnp.repeat(gmlp_bs[l].T, GMLP_GW, axis=1)
    wsx = jnp.repeat(jnp.transpose(gmlp_ws[l][:, :dseq, :dseq], (1, 2, 0)), GMLP_GW, axis=2)
    wsx = wsx.reshape(dseq * dseq, D_GMLP)
    bsx = bs_tile[:dseq]

    c_all = jnp.concatenate([c_prompt, c_sample], axis=0)
    mod = _modulation(c_all, w_ada[l], row(b_ada[l]))
    mod_p = mod[:bp].reshape(bp, N_MOD, 1, D_MODEL)
    mod_s = jnp.transpose(mod[bp:].reshape(bs, N_MOD, D_MODEL), (1, 0, 2))[None]

    xp = x_prompt.reshape(bp * seq, D_MODEL)
    xs = jnp.transpose(x_sample, (1, 0, 2)).reshape(dseq * bs, D_MODEL)
    n_s = dseq * bs
    keep = CONV_W - 1

    def ffn(x, modx, k, w_g, w_v, v_col0, w_down, tm, tf, rows_per_mod, emit_bf16=False):
        return _ffn(x, modx[:, 3 * k:3 * k + 3], w_g, w_v, v_col0, w_down, lng[k], lnb[k],
                    tm=tm, tf=tf, rows_per_mod=rows_per_mod, emit_bf16=emit_bf16)

    def merge(ya, yb, x1, modx, rows_per_mod, merge_w, cast_weights=None):
        return _merge(ya, yb, x1, modx[:, 3:6], *merge_w, lng[1], lnb[1],
                      tm=256, rows_per_mod=rows_per_mod, cast_weights=cast_weights)

    x1s, w_g_b, w_v_b, w_d_b = ffn(xs, mod_s, 0, ffn1_w_gu[l], ffn1_w_gu[l], D_FF, ffn1_w_down[l],
                                   n_s, 512, n_s, emit_bf16=True)
    (x1p,) = ffn(xp, mod_p, 0, w_g_b, w_v_b, 0, w_d_b, 1024, 512, seq)

    proj_s, w_branch = _sample_proj(x1s, mod_s[:, 3:6], w_in[l], gln_g, gln_b)
    sconv_tm = jnp.transpose(state_conv[l], (1, 0, 2))
    ya_s, yb_s, ht_s = _sample_mix(proj_s, sconv_tm, state_h[l], conv_w[l], cb, wbd, ba, bx, lam,
                                   wsx, bsx, nb=bs, nt=dseq)
    ya_p, yb_p, ht_p, tail_p, *merge_w = _prompt_mixer(
        x1p, mod_p[:, 3:6], w_branch, conv_w[l], cb, wbd, ba, bx, lam, gln_g, gln_b,
        gmlp_ws[l], bs_tile, batch=bp, seq=seq, tt=512,
        cast_weights=((w_in[l], 2 * D_MODEL, 1), (w_pa[l], D_MODEL, 0), (w_pb[l], D_MODEL, 0),
                      (w_out[l], D_MODEL, 0)))
    (x2s,) = merge(ya_s, yb_s, x1s, mod_s, n_s, merge_w)
    x2p, w_gu2_b, w_dn2_b = merge(ya_p, yb_p, x1p, mod_p, seq, merge_w,
                                  cast_weights=(ffn2_w_gu[l], ffn2_w_down[l]))

    (x3p,) = ffn(x2p, mod_p, 2, w_gu2_b, w_gu2_b, D_FF, w_dn2_b, 1024, 512, seq)
    (x3s,) = ffn(x2s, mod_s, 2, w_gu2_b, w_gu2_b, D_FF, w_dn2_b, n_s, 512, n_s)

    y_prompt = x3p.reshape(bp, seq, D_MODEL)
    y_sample = jnp.transpose(x3s.reshape(dseq, bs, D_MODEL), (1, 0, 2))
    new_conv_p = tail_p[:, V7X_SUBLANES - keep:][None]
    new_h_p = ht_p.reshape(bp, D_RNN)[None]
    xr_s = lax.slice(proj_s.reshape(dseq, bs, BRANCH_COLS), (dseq - keep, 0, 0), (dseq, bs, D_RNN))
    new_conv_s = jnp.transpose(xr_s, (1, 0, 2))[None]
    new_h_s = ht_s[None]
    vn_s = lax.slice(proj_s.reshape(dseq, bs, BRANCH_COLS), (0, 0, 2 * D_RNN + D_GMLP),
                     (dseq, bs, BRANCH_COLS))
    new_v_s = jnp.transpose(vn_s, (1, 0, 2))[None]
    return (y_prompt, y_sample, new_conv_p, new_h_p, new_conv_s, new_h_s, new_v_s)
```

```python
import functools
import math

import jax
import jax.numpy as jnp
from jax import lax
from jax.experimental import pallas as pl
from jax.experimental.pallas import tpu as pltpu

D_MODEL = 2048
D_RNN = 1024
RNN_BLOCKS = 16
RNN_BW = D_RNN // RNN_BLOCKS
CONV_W = 4
LRU_C = 8.0
D_GMLP = 1024
CHUNK = 128
GMLP_GROUPS = 8
GMLP_GW = D_GMLP // GMLP_GROUPS
D_FF = 5632
N_MOD = 9
IN_COLS = 2 * D_RNN + 2 * D_GMLP + 2 * D_MODEL
BRANCH_COLS = 2 * D_RNN + 2 * D_GMLP
LN_EPS = 1e-5
DEPTH = 1
ALPHA = (2.0 * DEPTH) ** 0.25

V7X_SUBLANES = 8
V7X_LANES = 128
V7X_MXU_DIM = 256
V7X_VMEM_BYTES = 64 * 1024 * 1024
V7X_VMEM_LIMIT_BYTES = 58 * 1024 * 1024
V7X_VMEM_FFN_LIMIT_BYTES = V7X_VMEM_BYTES - 512 * 1024

BF16 = jnp.bfloat16
F32 = jnp.float32


def _dot(a, b):
    return jnp.dot(a, b, preferred_element_type=F32)


def _layer_norm(x, g, b):
    mu = jnp.mean(x, axis=-1, keepdims=True)
    xc = x - mu
    var = jnp.mean(xc * xc, axis=-1, keepdims=True)
    return xc * lax.rsqrt(var + LN_EPS) * g + b


def _gelu_tanh(x):
    c = math.sqrt(2.0 / math.pi)
    return 0.5 * x * (1.0 + jnp.tanh(c * (x + 0.044715 * (x * x * x))))


def _softplus(x):
    return jnp.maximum(x, 0.0) + jnp.log1p(jnp.exp(-jnp.abs(x)))


def _params(*semantics, vmem_limit_bytes=V7X_VMEM_LIMIT_BYTES):
    return pltpu.CompilerParams(dimension_semantics=semantics, vmem_limit_bytes=vmem_limit_bytes)


_PROMPT_MOD_ROWS = V7X_SUBLANES


def _mod_spec(mod9, sub_layer, prompt):
    n_sample = mod9.shape[1] - _PROMPT_MOD_ROWS
    if prompt:
        return pl.BlockSpec((3, _PROMPT_MOD_ROWS, D_MODEL),
                            lambda i, *_: (sub_layer, n_sample // _PROMPT_MOD_ROWS, 0))
    return pl.BlockSpec((3, n_sample, D_MODEL), lambda i, *_: (sub_layer, 0, 0))


def _slabs(tm, mod9, prompt):
    n_sample = mod9.shape[1] - _PROMPT_MOD_ROWS
    if prompt:
        return [(0, tm)]
    return [(s, n_sample) for s in range(0, tm, n_sample)]


def _mod(mod_ref, c, row):
    return mod_ref[c] if row is None else mod_ref[c, pl.ds(row, 1), :]


def _modulated_bf16(x_ref, mod_ref, slabs, row):
    scale1, shift = 1.0 + _mod(mod_ref, 1, row), _mod(mod_ref, 0, row)
    parts = [(x_ref[r0:r0 + nr, :] * scale1 + shift).astype(BF16) for r0, nr in slabs]
    return parts[0] if len(parts) == 1 else jnp.concatenate(parts, axis=0)


def _mod_kernel(c_ref, w_ref, b_ref, o_ref):
    c = c_ref[...]
    a = (c * jax.nn.sigmoid(c)).astype(BF16)
    o_ref[0] = _dot(a, w_ref[...].astype(BF16)) + b_ref[...]


def _modulation(c_all, w_ada, b_ada):
    rows = c_all.shape[0]
    return pl.pallas_call(
        _mod_kernel,
        grid=(N_MOD,),
        in_specs=[
            pl.BlockSpec((rows, D_MODEL), lambda j: (0, 0)),
            pl.BlockSpec((D_MODEL, D_MODEL), lambda j: (0, j)),
            pl.BlockSpec((1, D_MODEL), lambda j: (0, j)),
        ],
        out_specs=pl.BlockSpec((1, rows, D_MODEL), lambda j: (j, 0, 0)),
        out_shape=jax.ShapeDtypeStruct((N_MOD, rows, D_MODEL), F32),
        compiler_params=_params("arbitrary"),
        name="adaln_mod",
    )(c_all, w_ada, b_ada)


def _ffn_kernel(x_ref, mod_ref, wg_ref, wv_ref, wd_ref, lng_ref, lnb_ref, o_ref, *rest,
                slabs, n_f, emit_bf16, tiles_per_batch):
    u_ref = rest[-1]
    j = pl.program_id(1)
    row = None if tiles_per_batch is None else pl.program_id(0) // tiles_per_batch

    def swiglu_down():
        if emit_bf16:
            wg, wv, wd = (r[...].astype(BF16) for r in (wg_ref, wv_ref, wd_ref))
            for dst, w in zip(rest[:3], (wg, wv, wd)):
                dst[...] = w
        else:
            wg, wv, wd = wg_ref[...], wv_ref[...], wd_ref[...]
        u = u_ref[...]
        g = _dot(u, wg)
        v = _dot(u, wv)
        h = (g * jax.nn.sigmoid(g) * v).astype(BF16)
        return _dot(h, wd)

    @pl.when(j == 0)
    def _():
        u_ref[...] = _modulated_bf16(x_ref, mod_ref, slabs, row)
        o_ref[...] = swiglu_down()

    @pl.when(jnp.logical_and(j > 0, j < n_f - 1))
    def _():
        o_ref[...] += swiglu_down()

    @pl.when(j == n_f - 1)
    def _():
        o_ref[...] += swiglu_down()
        for r0, nr in slabs:
            rows = pl.ds(r0, nr)
            y = ALPHA * x_ref[rows, :] + 0.5 * _mod(mod_ref, 2, row) * o_ref[rows, :]
            o_ref[rows, :] = _layer_norm(y, lng_ref[...], lnb_ref[...])


def _ffn(x, mod9, sub_layer, prompt_seq, w_g, w_v, v_col0, w_down, ln_g, ln_b, *, tm, tf,
         emit_bf16):
    rows = x.shape[0]
    prompt = prompt_seq is not None
    n_f = D_FF // tf
    v_blk0 = v_col0 // tf
    kern = functools.partial(_ffn_kernel, slabs=_slabs(tm, mod9, prompt), n_f=n_f,
                             emit_bf16=emit_bf16,
                             tiles_per_batch=prompt_seq // tm if prompt else None)
    up_spec = pl.BlockSpec((D_MODEL, tf), lambda i, j: (0, j))
    down_spec = pl.BlockSpec((tf, D_MODEL), lambda i, j: (j, 0))
    out_specs = [pl.BlockSpec((tm, D_MODEL), lambda i, j: (i, 0))]
    out_shape = [jax.ShapeDtypeStruct((rows, D_MODEL), F32)]
    if emit_bf16:
        pad = tf if rows > tm else 0
        emit_blk = lambda i, j: jnp.where(i == 0, j, n_f)
        out_specs += [pl.BlockSpec((D_MODEL, tf), lambda i, j: (0, emit_blk(i, j)))] * 2
        out_specs += [pl.BlockSpec((tf, D_MODEL), lambda i, j: (emit_blk(i, j), 0))]
        out_shape += [jax.ShapeDtypeStruct((D_MODEL, D_FF + pad), BF16)] * 2
        out_shape += [jax.ShapeDtypeStruct((D_FF + pad, D_MODEL), BF16)]
    return pl.pallas_call(
        kern,
        grid=(rows // tm, n_f),
        in_specs=[
            pl.BlockSpec((tm, D_MODEL), lambda i, j: (i, 0)),
            _mod_spec(mod9, sub_layer, prompt),
            up_spec,
            pl.BlockSpec((D_MODEL, tf), lambda i, j: (0, v_blk0 + j)),
            down_spec,
            pl.BlockSpec((1, D_MODEL), lambda i, j: (0, 0)),
            pl.BlockSpec((1, D_MODEL), lambda i, j: (0, 0)),
        ],
        out_specs=out_specs,
        out_shape=out_shape,
        scratch_shapes=[pltpu.VMEM((tm, D_MODEL), BF16)],
        compiler_params=_params("arbitrary", "arbitrary", vmem_limit_bytes=V7X_VMEM_FFN_LIMIT_BYTES),
        name="macaron_ffn",
    )(x, mod9, w_g, w_v, w_down, ln_g, ln_b)


def _lru_coeffs(xc, wbd_ref, ba, bx, lam):
    xcb = xc.astype(BF16)
    ra, rx = [], []
    for q in range(D_RNN // V7X_MXU_DIM):
        cols = slice(q * V7X_MXU_DIM, (q + 1) * V7X_MXU_DIM)
        rq = _dot(xcb[:, cols], wbd_ref[q])
        ra.append(rq[:, :V7X_MXU_DIM])
        rx.append(rq[:, V7X_MXU_DIM:])
    r = jax.nn.sigmoid(jnp.concatenate(ra, axis=-1) + ba)
    i = jax.nn.sigmoid(jnp.concatenate(rx, axis=-1) + bx)
    log_a = (-LRU_C * r) * _softplus(-lam)
    a = jnp.exp(log_a)
    z = -jnp.tanh(log_a) * (a * a + 1.0)
    mult = jnp.where(z > 0.0, z * lax.rsqrt(z), 0.0)
    return a, mult, i * xc


_MIXER_INPUTS = 13


def _prompt_mixer_kernel(*refs, tt, n_cast):
    (x_ref, mod_ref, w_ref, cw_ref, cb_ref, wbd_ref, ba_ref, bx_ref, lam_ref,
     glng_ref, glnb_ref, ws_ref, bs_ref) = refs[:_MIXER_INPUTS]
    cast_in = refs[_MIXER_INPUTS:_MIXER_INPUTS + n_cast]
    ya_ref, yb_ref, ht_ref, tailo_ref = refs[_MIXER_INPUTS + n_cast:_MIXER_INPUTS + n_cast + 4]
    cast_out = refs[_MIXER_INPUTS + n_cast + 4:_MIXER_INPUTS + 2 * n_cast + 4]
    tail_ref, h_ref, hs_ref = refs[_MIXER_INPUTS + 2 * n_cast + 4:]
    t = pl.program_id(1)
    ng = tt // V7X_SUBLANES

    for src, dst in zip(cast_in, cast_out):
        dst[...] = src[...].astype(BF16)

    @pl.when(t == 0)
    def _():
        tail_ref[...] = jnp.zeros_like(tail_ref)
        h_ref[...] = jnp.zeros_like(h_ref)

    u = _modulated_bf16(x_ref, mod_ref, [(0, tt)], pl.program_id(0))

    xr = _dot(u, w_ref[:, 0:D_RNN])
    gr = _dot(u, w_ref[:, D_RNN:2 * D_RNN])
    gu = _dot(u, w_ref[:, 2 * D_RNN:2 * D_RNN + D_GMLP])
    gv = _dot(u, w_ref[:, 2 * D_RNN + D_GMLP:BRANCH_COLS])

    prev = tail_ref[...]
    row = lax.broadcasted_iota(jnp.int32, (tt, D_RNN), 0)
    row8 = lax.broadcasted_iota(jnp.int32, (V7X_SUBLANES, D_RNN), 0)
    xc = cb_ref[...]
    for k in range(CONV_W):
        s = CONV_W - 1 - k
        if s == 0:
            xs = xr
        else:
            rolled = pltpu.roll(xr, s, 0)
            head = jnp.where(row8 < s, pltpu.roll(prev, s, 0), rolled[:V7X_SUBLANES])
            xs = jnp.concatenate([head, rolled[V7X_SUBLANES:]], axis=0)
        xc = xc + xs * cw_ref[k:k + 1, :]
    new_tail = xr[tt - V7X_SUBLANES:, :]
    tail_ref[...] = new_tail
    tailo_ref[0] = new_tail

    a, mult, gated = _lru_coeffs(xc, wbd_ref, ba_ref[...], bx_ref[...], lam_ref[...])
    mult = jnp.where(jnp.logical_and(row == 0, t == 0), 1.0, mult)
    b = mult * gated

    a = a.reshape(ng, V7X_SUBLANES, D_RNN)
    b = b.reshape(ng, V7X_SUBLANES, D_RNN)
    sub = lax.broadcasted_iota(jnp.int32, (ng, V7X_SUBLANES, D_RNN), 1)
    s = 1
    while s < V7X_SUBLANES:
        keep = sub >= s
        a_prev = jnp.where(keep, pltpu.roll(a, s, 1), 1.0)
        b_prev = jnp.where(keep, pltpu.roll(b, s, 1), 0.0)
        b = a * b_prev + b
        a = a * a_prev
        s *= 2
    h_in = h_ref[...]
    for g in range(ng):
        h = a[g] * h_in + b[g]
        hs_ref[g * V7X_SUBLANES:(g + 1) * V7X_SUBLANES, :] = h
        h_in = h[V7X_SUBLANES - 1:, :]
    h_ref[...] = h_in
    ht_ref[0] = h_in
    ya_ref[...] = (hs_ref[...] * _gelu_tanh(gr)).astype(BF16)

    vn = _layer_norm(gv, glng_ref[...], glnb_ref[...]).astype(BF16)
    ti = lax.broadcasted_iota(jnp.int32, (CHUNK, CHUNK), 0)
    si = lax.broadcasted_iota(jnp.int32, (CHUNK, CHUNK), 1)
    causal = si <= ti
    for g in range(GMLP_GROUPS):
        w = jnp.where(causal, ws_ref[g], 0.0).astype(BF16)
        c0 = g * GMLP_GW
        bias = bs_ref[:, c0:c0 + GMLP_GW]
        n_chunks = tt // CHUNK
        rhs = jnp.concatenate([vn[c * CHUNK:(c + 1) * CHUNK, c0:c0 + GMLP_GW]
                               for c in range(n_chunks)], axis=1)
        s_all = _dot(w, rhs)
        for c in range(n_chunks):
            r0 = c * CHUNK
            s_blk = s_all[:, c * GMLP_GW:(c + 1) * GMLP_GW] + bias
            yb_ref[r0:r0 + CHUNK, c0:c0 + GMLP_GW] = (
                gu[r0:r0 + CHUNK, c0:c0 + GMLP_GW] * s_blk).astype(BF16)


def _prompt_mixer(x, mod9, w_branch, conv_w, conv_b, wbd, ba, bx, lam, gln_g, gln_b, ws, bs_tile,
                  *, batch, seq, tt, cast_weights=()):
    nt = seq // tt
    steps = batch * nt
    kern = functools.partial(_prompt_mixer_kernel, tt=tt, n_cast=len(cast_weights))
    vec = pl.BlockSpec((1, D_RNN), lambda b, t: (0, 0))
    cast_in_specs = [pl.BlockSpec((w.shape[0] // steps, nc), lambda b, t, cb=cb: (b * nt + t, cb))
                     for w, nc, cb in cast_weights]
    cast_out_specs = [pl.BlockSpec((w.shape[0] // steps, nc), lambda b, t: (b * nt + t, 0))
                      for w, nc, cb in cast_weights]
    return pl.pallas_call(
        kern,
        grid=(batch, nt),
        in_specs=[
            pl.BlockSpec((tt, D_MODEL), lambda b, t: (b * nt + t, 0)),
            _mod_spec(mod9, 1, True),
            pl.BlockSpec((D_MODEL, BRANCH_COLS), lambda b, t: (0, 0)),
            pl.BlockSpec((CONV_W, D_RNN), lambda b, t: (0, 0)),
            vec,
            pl.BlockSpec((D_RNN // V7X_MXU_DIM, V7X_MXU_DIM, 2 * V7X_MXU_DIM), lambda b, t: (0, 0, 0)),
            vec, vec, vec, vec, vec,
            pl.BlockSpec((GMLP_GROUPS, CHUNK, CHUNK), lambda b, t: (0, 0, 0)),
            pl.BlockSpec((CHUNK, D_GMLP), lambda b, t: (0, 0)),
        ] + cast_in_specs,
        out_specs=[
            pl.BlockSpec((tt, D_RNN), lambda b, t: (b * nt + t, 0)),
            pl.BlockSpec((tt, D_GMLP), lambda b, t: (b * nt + t, 0)),
            pl.BlockSpec((1, 1, D_RNN), lambda b, t: (b, 0, 0)),
            pl.BlockSpec((1, V7X_SUBLANES, D_RNN), lambda b, t: (b, 0, 0)),
        ] + cast_out_specs,
        out_shape=[
            jax.ShapeDtypeStruct((batch * seq, D_RNN), BF16),
            jax.ShapeDtypeStruct((batch * seq, D_GMLP), BF16),
            jax.ShapeDtypeStruct((batch, 1, D_RNN), F32),
            jax.ShapeDtypeStruct((batch, V7X_SUBLANES, D_RNN), F32),
        ] + [jax.ShapeDtypeStruct((w.shape[0], nc), BF16) for w, nc, cb in cast_weights],
        scratch_shapes=[pltpu.VMEM((V7X_SUBLANES, D_RNN), F32), pltpu.VMEM((1, D_RNN), F32),
                        pltpu.VMEM((tt, D_RNN), F32)],
        compiler_params=_params("arbitrary", "arbitrary"),
        name="prompt_mixer",
    )(x, mod9, w_branch, conv_w, conv_b, wbd, ba, bx, lam, gln_g, gln_b, ws, bs_tile,
      *[w for w, nc, cb in cast_weights])


_PROJ_TN = 1024


def _sample_proj_kernel(x_ref, mod_ref, w_ref, lng_ref, lnb_ref, o_ref, wb_ref, u_ref, *, slabs):
    j = pl.program_id(0)

    @pl.when(j == 0)
    def _():
        u_ref[...] = _modulated_bf16(x_ref, mod_ref, slabs, None)

    wb_ref[...] = w_ref[...].astype(BF16)
    r = _dot(u_ref[...], wb_ref[...])

    @pl.when(jnp.logical_or(j == 0, j == 2))
    def _():
        o_ref[...] = r

    @pl.when(j == 1)
    def _():
        o_ref[...] = _gelu_tanh(r)

    @pl.when(j == 3)
    def _():
        o_ref[...] = _layer_norm(r, lng_ref[...], lnb_ref[...])


def _sample_proj(x, mod9, w_in, gln_g, gln_b):
    rows = x.shape[0]
    kern = functools.partial(_sample_proj_kernel, slabs=_slabs(rows, mod9, False))
    w_spec = pl.BlockSpec((D_MODEL, _PROJ_TN), lambda j: (0, j))
    return pl.pallas_call(
        kern,
        grid=(BRANCH_COLS // _PROJ_TN,),
        in_specs=[
            pl.BlockSpec((rows, D_MODEL), lambda j: (0, 0)),
            _mod_spec(mod9, 1, False),
            w_spec,
            pl.BlockSpec((1, D_GMLP), lambda j: (0, 0)),
            pl.BlockSpec((1, D_GMLP), lambda j: (0, 0)),
        ],
        out_specs=[pl.BlockSpec((rows, _PROJ_TN), lambda j: (0, j)), w_spec],
        out_shape=[jax.ShapeDtypeStruct((rows, BRANCH_COLS), F32),
                   jax.ShapeDtypeStruct((D_MODEL, BRANCH_COLS), BF16)],
        scratch_shapes=[pltpu.VMEM((rows, D_MODEL), BF16)],
        compiler_params=_params("arbitrary"),
        name="sample_proj",
    )(x, mod9, w_in, gln_g, gln_b)


def _sample_mix_kernel(xr_ref, gg_ref, gu_ref, vn_ref, sconv_ref, h0_ref, cw_ref, cb_ref,
                       wbd_ref, ba_ref, bx_ref, lam_ref, wsx_ref, bsx_ref,
                       ya_ref, yb_ref, ht_ref, xc_ref, *, nb, nt):
    def slab(t):
        return pl.ds(t * nb, nb)

    xpad = [sconv_ref[k] for k in range(CONV_W - 1)] + [xr_ref[slab(t), :] for t in range(nt)]
    for t in range(nt):
        xc = cb_ref[...]
        for k in range(CONV_W):
            xc = xc + xpad[t + k] * cw_ref[k:k + 1, :]
        xc_ref[slab(t), :] = xc

    a, mult, gated = _lru_coeffs(xc_ref[...], wbd_ref, ba_ref[...], bx_ref[...], lam_ref[...])
    b = mult * gated
    h = h0_ref[...]
    for t in range(nt):
        lo, hi = t * nb, (t + 1) * nb
        h = a[lo:hi] * h + b[lo:hi]
        ya_ref[slab(t), :] = (h * gg_ref[slab(t), :]).astype(BF16)
    ht_ref[...] = h

    for t in range(nt):
        s = bsx_ref[t:t + 1, :]
        for sp in range(t + 1):
            s = s + wsx_ref[t * nt + sp:t * nt + sp + 1, :] * vn_ref[slab(sp), :]
        yb_ref[slab(t), :] = (gu_ref[slab(t), :] * s).astype(BF16)


def _sample_mix(proj, sconv_tm, h0, conv_w, conv_b, wbd, ba, bx, lam, wsx, bsx, *, nb, nt):
    rows = nb * nt
    kern = functools.partial(_sample_mix_kernel, nb=nb, nt=nt)
    vec = pl.BlockSpec((1, D_RNN), lambda i: (0, 0))

    def col(jb):
        return pl.BlockSpec((rows, D_RNN), lambda i: (0, jb))

    return pl.pallas_call(
        kern,
        grid=(1,),
        in_specs=[
            col(0), col(1), col(2), col(3),
            pl.BlockSpec((CONV_W - 1, nb, D_RNN), lambda i: (0, 0, 0)),
            pl.BlockSpec((nb, D_RNN), lambda i: (0, 0)),
            pl.BlockSpec((CONV_W, D_RNN), lambda i: (0, 0)),
            vec,
            pl.BlockSpec((D_RNN // V7X_MXU_DIM, V7X_MXU_DIM, 2 * V7X_MXU_DIM), lambda i: (0, 0, 0)),
            vec, vec, vec,
            pl.BlockSpec((nt * nt, D_GMLP), lambda i: (0, 0)),
            pl.BlockSpec((nt, D_GMLP), lambda i: (0, 0)),
        ],
        out_specs=[
            pl.BlockSpec((rows, D_RNN), lambda i: (0, 0)),
            pl.BlockSpec((rows, D_GMLP), lambda i: (0, 0)),
            pl.BlockSpec((nb, D_RNN), lambda i: (0, 0)),
        ],
        out_shape=[
            jax.ShapeDtypeStruct((rows, D_RNN), BF16),
            jax.ShapeDtypeStruct((rows, D_GMLP), BF16),
            jax.ShapeDtypeStruct((nb, D_RNN), F32),
        ],
        scratch_shapes=[pltpu.VMEM((rows, D_RNN), F32)],
        compiler_params=_params("arbitrary"),
        name="sample_mix",
    )(proj, proj, proj, proj, sconv_tm, h0, conv_w, conv_b, wbd, ba, bx, lam, wsx, bsx)


def _merge_kernel(ya_ref, yb_ref, x_ref, mod_ref, wgate_ref, wpa_ref, wpb_ref, wo_ref,
                  lng_ref, lnb_ref, *rest, slabs, tiles_per_batch):
    row = None if tiles_per_batch is None else pl.program_id(0) // tiles_per_batch
    if len(rest) == 1:
        (o_ref,) = rest
    else:
        wgu_ref, wdn_ref, o_ref, wgu_b_ref, wdn_b_ref = rest
        wgu_b_ref[...] = wgu_ref[...].astype(BF16)
        wdn_b_ref[...] = wdn_ref[...].astype(BF16)
    u = _modulated_bf16(x_ref, mod_ref, slabs, row)
    gate = _mod(mod_ref, 2, row)
    y_a = jax.nn.sigmoid(_dot(u, wgate_ref[:, :D_MODEL])) * _dot(ya_ref[...], wpa_ref[...])
    y_b = jax.nn.sigmoid(_dot(u, wgate_ref[:, D_MODEL:])) * _dot(yb_ref[...], wpb_ref[...])
    mix = _dot((y_a + y_b).astype(BF16), wo_ref[...])
    for r0, nr in slabs:
        y = ALPHA * x_ref[r0:r0 + nr, :] + gate * mix[r0:r0 + nr, :]
        o_ref[r0:r0 + nr, :] = _layer_norm(y, lng_ref[...], lnb_ref[...])


def _merge(ya, yb, x, mod9, prompt_seq, w_gate, w_pa, w_pb, w_out, ln_g, ln_b, *, tm,
           cast_weights=None):
    rows = x.shape[0]
    prompt = prompt_seq is not None
    steps = rows // tm
    kern = functools.partial(_merge_kernel, slabs=_slabs(tm, mod9, prompt),
                             tiles_per_batch=prompt_seq // tm if prompt else None)
    whole = lambda shape: pl.BlockSpec(shape, lambda i: (0, 0))
    in_specs = [
        pl.BlockSpec((tm, D_RNN), lambda i: (i, 0)),
        pl.BlockSpec((tm, D_GMLP), lambda i: (i, 0)),
        pl.BlockSpec((tm, D_MODEL), lambda i: (i, 0)),
        _mod_spec(mod9, 1, prompt),
        whole((D_MODEL, 2 * D_MODEL)),
        whole((D_RNN, D_MODEL)),
        whole((D_GMLP, D_MODEL)),
        whole((D_MODEL, D_MODEL)),
        whole((1, D_MODEL)),
        whole((1, D_MODEL)),
    ]
    args = [ya, yb, x, mod9, w_gate, w_pa, w_pb, w_out, ln_g, ln_b]
    out_specs = [pl.BlockSpec((tm, D_MODEL), lambda i: (i, 0))]
    out_shape = [jax.ShapeDtypeStruct((rows, D_MODEL), F32)]
    if cast_weights is not None:
        for w in cast_weights:
            slab = pl.BlockSpec((w.shape[0] // steps, w.shape[1]), lambda i: (i, 0))
            in_specs.append(slab)
            args.append(w)
            out_specs.append(slab)
            out_shape.append(jax.ShapeDtypeStruct(w.shape, BF16))
    return pl.pallas_call(
        kern,
        grid=(steps,),
        in_specs=in_specs,
        out_specs=out_specs,
        out_shape=out_shape,
        compiler_params=_params("arbitrary", vmem_limit_bytes=V7X_VMEM_FFN_LIMIT_BYTES),
        name="merge_out",
    )(*args)


def _block_diag_tiles(w):
    per = V7X_MXU_DIM // RNN_BW
    w4 = w.reshape(RNN_BLOCKS // per, per, RNN_BW, RNN_BW)
    eye = jnp.eye(per, dtype=w.dtype)
    t = w4[:, :, :, None, :] * eye[None, :, None, :, None]
    return t.reshape(RNN_BLOCKS // per, V7X_MXU_DIM, V7X_MXU_DIM)


def kernel(x_prompt, x_sample, state_conv, state_h, c_prompt, c_sample, w_ada, b_ada, ffn1_w_gu, ffn1_w_down, ffn2_w_gu, ffn2_w_down, w_in, conv_w, conv_b, lru_wa, lru_ba, lru_wx, lru_bx, lru_lambda, gmlp_ln_g, gmlp_ln_b, gmlp_ws, gmlp_bs, w_pa, w_pb, w_out, ln_g, ln_b):
    assert w_ada.shape[0] == DEPTH == 1
    bp, seq, _ = x_prompt.shape
    bs, dseq, _ = x_sample.shape
    l = 0

    wbd = jnp.concatenate([_block_diag_tiles(lru_wa[l]), _block_diag_tiles(lru_wx[l])],
                          axis=-1).astype(BF16)

    row = lambda v: v.reshape(1, -1)
    cb, ba, bx, lam = row(conv_b[l]), row(lru_ba[l]), row(lru_bx[l]), row(lru_lambda[l])
    gln_g, gln_b = row(gmlp_ln_g[l]), row(gmlp_ln_b[l])
    lng = [row(ln_g[l, k]) for k in range(3)]
    lnb = [row(ln_b[l, k]) for k in range(3)]

    bs_tile = jnp.repeat(gmlp_bs[l].T, GMLP_GW, axis=1)
    wsx = jnp.repeat(jnp.transpose(gmlp_ws[l][:, :dseq, :dseq], (1, 2, 0)), GMLP_GW, axis=2)
    wsx = wsx.reshape(dseq * dseq, D_GMLP)
    bsx = bs_tile[:dseq]

    assert bs % V7X_SUBLANES == 0 and bp <= _PROMPT_MOD_ROWS
    c_all = jnp.concatenate(
        [c_sample, c_prompt, jnp.zeros((_PROMPT_MOD_ROWS - bp, D_MODEL), c_prompt.dtype)], axis=0)
    mod9 = _modulation(c_all, w_ada[l], row(b_ada[l]))

    xp = x_prompt.reshape(bp * seq, D_MODEL)
    xs = jnp.transpose(x_sample, (1, 0, 2)).reshape(dseq * bs, D_MODEL)
    n_s = dseq * bs
    keep = CONV_W - 1

    def ffn(x, prompt_seq, k, w_g, w_v, v_col0, w_down, tm, tf, emit_bf16=False):
        return _ffn(x, mod9, k, prompt_seq, w_g, w_v, v_col0, w_down, lng[k], lnb[k],
                    tm=tm, tf=tf, emit_bf16=emit_bf16)

    def merge(ya, yb, x1, prompt_seq, merge_w, cast_weights=None):
        return _merge(ya, yb, x1, mod9, prompt_seq, *merge_w, lng[1], lnb[1],
                      tm=256, cast_weights=cast_weights)

    x1s, w_g_b, w_v_b, w_d_b = ffn(xs, None, 0, ffn1_w_gu[l], ffn1_w_gu[l], D_FF, ffn1_w_down[l],
                                   n_s, 512, emit_bf16=True)
    (x1p,) = ffn(xp, seq, 0, w_g_b, w_v_b, 0, w_d_b, 1024, 512)

    proj_s, w_branch = _sample_proj(x1s, mod9, w_in[l], gln_g, gln_b)
    sconv_tm = jnp.transpose(state_conv[l], (1, 0, 2))
    ya_s, yb_s, ht_s = _sample_mix(proj_s, sconv_tm, state_h[l], conv_w[l], cb, wbd, ba, bx, lam,
                                   wsx, bsx, nb=bs, nt=dseq)
    ya_p, yb_p, ht_p, tail_p, *merge_w = _prompt_mixer(
        x1p, mod9, w_branch, conv_w[l], cb, wbd, ba, bx, lam, gln_g, gln_b,
        gmlp_ws[l], bs_tile, batch=bp, seq=seq, tt=512,
        cast_weights=((w_in[l], 2 * D_MODEL, 1), (w_pa[l], D_MODEL, 0), (w_pb[l], D_MODEL, 0),
                      (w_out[l], D_MODEL, 0)))
    (x2s,) = merge(ya_s, yb_s, x1s, None, merge_w)
    x2p, w_gu2_b, w_dn2_b = merge(ya_p, yb_p, x1p, seq, merge_w,
                                  cast_weights=(ffn2_w_gu[l], ffn2_w_down[l]))

    (x3p,) = ffn(x2p, seq, 2, w_gu2_b, w_gu2_b, D_FF, w_dn2_b, 1024, 512)
    (x3s,) = ffn(x2s, None, 2, w_gu2_b, w_gu2_b, D_FF, w_dn2_b, n_s, 512)

    y_prompt = x3p.reshape(bp, seq, D_MODEL)
    y_sample = jnp.transpose(x3s.reshape(dseq, bs, D_MODEL), (1, 0, 2))
    new_conv_p = tail_p[:, V7X_SUBLANES - keep:][None]
    new_h_p = ht_p.reshape(bp, D_RNN)[None]
    xr_s = lax.slice(proj_s.reshape(dseq, bs, BRANCH_COLS), (dseq - keep, 0, 0), (dseq, bs, D_RNN))
    new_conv_s = jnp.transpose(xr_s, (1, 0, 2))[None]
    new_h_s = ht_s[None]
    vn_s = lax.slice(proj_s.reshape(dseq, bs, BRANCH_COLS), (0, 0, 2 * D_RNN + D_GMLP),
                     (dseq, bs, BRANCH_COLS))
    new_v_s = jnp.transpose(vn_s, (1, 0, 2))[None]
    return (y_prompt, y_sample, new_conv_p, new_h_p, new_conv_s, new_h_s, new_v_s)
```

```python
import functools
import math

import jax
import jax.numpy as jnp
from jax import lax
from jax.experimental import pallas as pl
from jax.experimental.pallas import tpu as pltpu

D_MODEL = 2048
D_RNN = 1024
RNN_BLOCKS = 16
RNN_BW = D_RNN // RNN_BLOCKS
CONV_W = 4
LRU_C = 8.0
D_GMLP = 1024
CHUNK = 128
GMLP_GROUPS = 8
GMLP_GW = D_GMLP // GMLP_GROUPS
D_FF = 5632
N_MOD = 9
IN_COLS = 2 * D_RNN + 2 * D_GMLP + 2 * D_MODEL
BRANCH_COLS = 2 * D_RNN + 2 * D_GMLP
LN_EPS = 1e-5
DEPTH = 1
ALPHA = (2.0 * DEPTH) ** 0.25

V7X_SUBLANES = 8
V7X_LANES = 128
V7X_MXU_DIM = 256
V7X_VMEM_BYTES = 64 * 1024 * 1024
V7X_VMEM_LIMIT_BYTES = 58 * 1024 * 1024
V7X_VMEM_FFN_LIMIT_BYTES = V7X_VMEM_BYTES - 512 * 1024

BF16 = jnp.bfloat16
F32 = jnp.float32


def _dot(a, b):
    return jnp.dot(a, b, preferred_element_type=F32)


def _layer_norm(x, g, b):
    mu = jnp.mean(x, axis=-1, keepdims=True)
    xc = x - mu
    var = jnp.mean(xc * xc, axis=-1, keepdims=True)
    return xc * lax.rsqrt(var + LN_EPS) * g + b


def _gelu_tanh(x):
    c = math.sqrt(2.0 / math.pi)
    return 0.5 * x * (1.0 + jnp.tanh(c * (x + 0.044715 * (x * x * x))))


def _softplus(x):
    return jnp.maximum(x, 0.0) + jnp.log1p(jnp.exp(-jnp.abs(x)))


def _params(*semantics, vmem_limit_bytes=V7X_VMEM_LIMIT_BYTES):
    return pltpu.CompilerParams(dimension_semantics=semantics, vmem_limit_bytes=vmem_limit_bytes)


_PROMPT_MOD_ROWS = V7X_SUBLANES


def _mod_spec(mod9, sub_layer, prompt):
    n_sample = mod9.shape[1] - _PROMPT_MOD_ROWS
    if prompt:
        return pl.BlockSpec((3, _PROMPT_MOD_ROWS, D_MODEL),
                            lambda i, *_: (sub_layer, n_sample // _PROMPT_MOD_ROWS, 0))
    return pl.BlockSpec((3, n_sample, D_MODEL), lambda i, *_: (sub_layer, 0, 0))


def _slabs(tm, mod9, prompt):
    n_sample = mod9.shape[1] - _PROMPT_MOD_ROWS
    if prompt:
        return [(0, tm)]
    return [(s, n_sample) for s in range(0, tm, n_sample)]


def _mod(mod_ref, c, row):
    return mod_ref[c] if row is None else mod_ref[c, pl.ds(row, 1), :]


def _modulated_bf16(x_ref, mod_ref, slabs, row):
    scale1, shift = 1.0 + _mod(mod_ref, 1, row), _mod(mod_ref, 0, row)
    parts = [(x_ref[r0:r0 + nr, :] * scale1 + shift).astype(BF16) for r0, nr in slabs]
    return parts[0] if len(parts) == 1 else jnp.concatenate(parts, axis=0)


def _mod_kernel(c_ref, w_ref, b_ref, o_ref):
    c = c_ref[...]
    a = (c * jax.nn.sigmoid(c)).astype(BF16)
    o_ref[0] = _dot(a, w_ref[...].astype(BF16)) + b_ref[...]


def _modulation(c_all, w_ada, b_ada):
    rows = c_all.shape[0]
    return pl.pallas_call(
        _mod_kernel,
        grid=(N_MOD,),
        in_specs=[
            pl.BlockSpec((rows, D_MODEL), lambda j: (0, 0)),
            pl.BlockSpec((D_MODEL, D_MODEL), lambda j: (0, j)),
            pl.BlockSpec((1, D_MODEL), lambda j: (0, j)),
        ],
        out_specs=pl.BlockSpec((1, rows, D_MODEL), lambda j: (j, 0, 0)),
        out_shape=jax.ShapeDtypeStruct((N_MOD, rows, D_MODEL), F32),
        compiler_params=_params("arbitrary"),
        name="adaln_mod",
    )(c_all, w_ada, b_ada)


def _ffn_kernel(x_ref, mod_ref, wg_ref, wv_ref, wd_ref, lng_ref, lnb_ref, *rest,
                slabs, n_f, emit_bf16, tiles_per_batch, n_cast):
    cast_in, o_ref = rest[:n_cast], rest[n_cast]
    emit_out = rest[n_cast + 1:n_cast + 4] if emit_bf16 else ()
    cast_out = rest[n_cast + 1 + len(emit_out):-1]
    u_ref = rest[-1]
    j = pl.program_id(1)
    row = None if tiles_per_batch is None else pl.program_id(0) // tiles_per_batch

    def swiglu_down():
        for src, dst in zip(cast_in, cast_out):
            dst[...] = src[...].astype(BF16)
        if emit_bf16:
            wg, wv, wd = (r[...].astype(BF16) for r in (wg_ref, wv_ref, wd_ref))
            for dst, w in zip(emit_out, (wg, wv, wd)):
                dst[...] = w
        else:
            wg, wv, wd = wg_ref[...], wv_ref[...], wd_ref[...]
        u = u_ref[...]
        g = _dot(u, wg)
        v = _dot(u, wv)
        h = (g * jax.nn.sigmoid(g) * v).astype(BF16)
        return _dot(h, wd)

    @pl.when(j == 0)
    def _():
        u_ref[...] = _modulated_bf16(x_ref, mod_ref, slabs, row)
        o_ref[...] = swiglu_down()

    @pl.when(jnp.logical_and(j > 0, j < n_f - 1))
    def _():
        o_ref[...] += swiglu_down()

    @pl.when(j == n_f - 1)
    def _():
        o_ref[...] += swiglu_down()
        for r0, nr in slabs:
            rows = pl.ds(r0, nr)
            y = ALPHA * x_ref[rows, :] + 0.5 * _mod(mod_ref, 2, row) * o_ref[rows, :]
            o_ref[rows, :] = _layer_norm(y, lng_ref[...], lnb_ref[...])


def _ffn(x, mod9, sub_layer, prompt_seq, w_g, w_v, v_col0, w_down, ln_g, ln_b, *, tm, tf,
         emit_bf16, cast_weights=()):
    rows = x.shape[0]
    prompt = prompt_seq is not None
    n_f = D_FF // tf
    v_blk0 = v_col0 // tf
    steps = (rows // tm) * n_f
    kern = functools.partial(_ffn_kernel, slabs=_slabs(tm, mod9, prompt), n_f=n_f,
                             emit_bf16=emit_bf16,
                             tiles_per_batch=prompt_seq // tm if prompt else None,
                             n_cast=len(cast_weights))
    up_spec = pl.BlockSpec((D_MODEL, tf), lambda i, j: (0, j))
    down_spec = pl.BlockSpec((tf, D_MODEL), lambda i, j: (j, 0))
    out_specs = [pl.BlockSpec((tm, D_MODEL), lambda i, j: (i, 0))]
    out_shape = [jax.ShapeDtypeStruct((rows, D_MODEL), F32)]
    cast_specs = []
    for w, slab_rows in cast_weights:
        n_slabs = w.shape[0] // slab_rows
        assert n_slabs * slab_rows == w.shape[0] and n_slabs <= steps
        cast_specs.append(pl.BlockSpec(
            (slab_rows, w.shape[1]),
            lambda i, j, n_slabs=n_slabs: (jnp.minimum(i * n_f + j, n_slabs - 1), 0)))
    if emit_bf16:
        pad = tf if rows > tm else 0
        emit_blk = lambda i, j: jnp.where(i == 0, j, n_f)
        out_specs += [pl.BlockSpec((D_MODEL, tf), lambda i, j: (0, emit_blk(i, j)))] * 2
        out_specs += [pl.BlockSpec((tf, D_MODEL), lambda i, j: (emit_blk(i, j), 0))]
        out_shape += [jax.ShapeDtypeStruct((D_MODEL, D_FF + pad), BF16)] * 2
        out_shape += [jax.ShapeDtypeStruct((D_FF + pad, D_MODEL), BF16)]
    out_specs += cast_specs
    out_shape += [jax.ShapeDtypeStruct(w.shape, BF16) for w, _ in cast_weights]
    return pl.pallas_call(
        kern,
        grid=(rows // tm, n_f),
        in_specs=[
            pl.BlockSpec((tm, D_MODEL), lambda i, j: (i, 0)),
            _mod_spec(mod9, sub_layer, prompt),
            up_spec,
            pl.BlockSpec((D_MODEL, tf), lambda i, j: (0, v_blk0 + j)),
            down_spec,
            pl.BlockSpec((1, D_MODEL), lambda i, j: (0, 0)),
            pl.BlockSpec((1, D_MODEL), lambda i, j: (0, 0)),
        ] + cast_specs,
        out_specs=out_specs,
        out_shape=out_shape,
        scratch_shapes=[pltpu.VMEM((tm, D_MODEL), BF16)],
        compiler_params=_params("arbitrary", "arbitrary", vmem_limit_bytes=V7X_VMEM_FFN_LIMIT_BYTES),
        name="macaron_ffn",
    )(x, mod9, w_g, w_v, w_down, ln_g, ln_b, *[w for w, _ in cast_weights])


def _lru_coeffs(xc, wbd_ref, ba, bx, lam):
    xcb = xc.astype(BF16)
    ra, rx = [], []
    for q in range(D_RNN // V7X_MXU_DIM):
        cols = slice(q * V7X_MXU_DIM, (q + 1) * V7X_MXU_DIM)
        rq = _dot(xcb[:, cols], wbd_ref[q])
        ra.append(rq[:, :V7X_MXU_DIM])
        rx.append(rq[:, V7X_MXU_DIM:])
    r = jax.nn.sigmoid(jnp.concatenate(ra, axis=-1) + ba)
    i = jax.nn.sigmoid(jnp.concatenate(rx, axis=-1) + bx)
    log_a = (-LRU_C * r) * _softplus(-lam)
    a = jnp.exp(log_a)
    z = -jnp.tanh(log_a) * (a * a + 1.0)
    mult = jnp.where(z > 0.0, z * lax.rsqrt(z), 0.0)
    return a, mult, i * xc


_MIXER_INPUTS = 13


def _prompt_mixer_kernel(*refs, tt, n_cast):
    (x_ref, mod_ref, w_ref, cw_ref, cb_ref, wbd_ref, ba_ref, bx_ref, lam_ref,
     glng_ref, glnb_ref, ws_ref, bs_ref) = refs[:_MIXER_INPUTS]
    cast_in = refs[_MIXER_INPUTS:_MIXER_INPUTS + n_cast]
    ya_ref, yb_ref, ht_ref, tailo_ref = refs[_MIXER_INPUTS + n_cast:_MIXER_INPUTS + n_cast + 4]
    cast_out = refs[_MIXER_INPUTS + n_cast + 4:_MIXER_INPUTS + 2 * n_cast + 4]
    tail_ref, h_ref, hs_ref = refs[_MIXER_INPUTS + 2 * n_cast + 4:]
    t = pl.program_id(1)
    ng = tt // V7X_SUBLANES

    for src, dst in zip(cast_in, cast_out):
        dst[...] = src[...].astype(BF16)

    @pl.when(t == 0)
    def _():
        tail_ref[...] = jnp.zeros_like(tail_ref)
        h_ref[...] = jnp.zeros_like(h_ref)

    u = _modulated_bf16(x_ref, mod_ref, [(0, tt)], pl.program_id(0))

    xr = _dot(u, w_ref[:, 0:D_RNN])
    gr = _dot(u, w_ref[:, D_RNN:2 * D_RNN])
    gu = _dot(u, w_ref[:, 2 * D_RNN:2 * D_RNN + D_GMLP])
    gv = _dot(u, w_ref[:, 2 * D_RNN + D_GMLP:BRANCH_COLS])

    prev = tail_ref[...]
    row = lax.broadcasted_iota(jnp.int32, (tt, D_RNN), 0)
    row8 = lax.broadcasted_iota(jnp.int32, (V7X_SUBLANES, D_RNN), 0)
    xc = cb_ref[...]
    for k in range(CONV_W):
        s = CONV_W - 1 - k
        if s == 0:
            xs = xr
        else:
            rolled = pltpu.roll(xr, s, 0)
            head = jnp.where(row8 < s, pltpu.roll(prev, s, 0), rolled[:V7X_SUBLANES])
            xs = jnp.concatenate([head, rolled[V7X_SUBLANES:]], axis=0)
        xc = xc + xs * cw_ref[k:k + 1, :]
    new_tail = xr[tt - V7X_SUBLANES:, :]
    tail_ref[...] = new_tail
    tailo_ref[0] = new_tail

    a, mult, gated = _lru_coeffs(xc, wbd_ref, ba_ref[...], bx_ref[...], lam_ref[...])
    mult = jnp.where(jnp.logical_and(row == 0, t == 0), 1.0, mult)
    b = mult * gated

    a = a.reshape(ng, V7X_SUBLANES, D_RNN)
    b = b.reshape(ng, V7X_SUBLANES, D_RNN)
    sub = lax.broadcasted_iota(jnp.int32, (ng, V7X_SUBLANES, D_RNN), 1)
    s = 1
    while s < V7X_SUBLANES:
        keep = sub >= s
        a_prev = jnp.where(keep, pltpu.roll(a, s, 1), 1.0)
        b_prev = jnp.where(keep, pltpu.roll(b, s, 1), 0.0)
        b = a * b_prev + b
        a = a * a_prev
        s *= 2
    h_in = h_ref[...]
    for g in range(ng):
        h = a[g] * h_in + b[g]
        hs_ref[g * V7X_SUBLANES:(g + 1) * V7X_SUBLANES, :] = h
        h_in = h[V7X_SUBLANES - 1:, :]
    h_ref[...] = h_in
    ht_ref[0] = h_in
    ya_ref[...] = (hs_ref[...] * _gelu_tanh(gr)).astype(BF16)

    vn = _layer_norm(gv, glng_ref[...], glnb_ref[...]).astype(BF16)
    ti = lax.broadcasted_iota(jnp.int32, (CHUNK, CHUNK), 0)
    si = lax.broadcasted_iota(jnp.int32, (CHUNK, CHUNK), 1)
    causal = si <= ti
    for g in range(GMLP_GROUPS):
        w = jnp.where(causal, ws_ref[g], 0.0).astype(BF16)
        c0 = g * GMLP_GW
        bias = bs_ref[:, c0:c0 + GMLP_GW]
        n_chunks = tt // CHUNK
        rhs = jnp.concatenate([vn[c * CHUNK:(c + 1) * CHUNK, c0:c0 + GMLP_GW]
                               for c in range(n_chunks)], axis=1)
        s_all = _dot(w, rhs)
        for c in range(n_chunks):
            r0 = c * CHUNK
            s_blk = s_all[:, c * GMLP_GW:(c + 1) * GMLP_GW] + bias
            yb_ref[r0:r0 + CHUNK, c0:c0 + GMLP_GW] = (
                gu[r0:r0 + CHUNK, c0:c0 + GMLP_GW] * s_blk).astype(BF16)


def _prompt_mixer(x, mod9, w_branch, conv_w, conv_b, wbd, ba, bx, lam, gln_g, gln_b, ws, bs_tile,
                  *, batch, seq, tt, cast_weights=()):
    nt = seq // tt
    steps = batch * nt
    kern = functools.partial(_prompt_mixer_kernel, tt=tt, n_cast=len(cast_weights))
    vec = pl.BlockSpec((1, D_RNN), lambda b, t: (0, 0))
    cast_in_specs = [pl.BlockSpec((w.shape[0] // steps, nc), lambda b, t, cb=cb: (b * nt + t, cb))
                     for w, nc, cb in cast_weights]
    cast_out_specs = [pl.BlockSpec((w.shape[0] // steps, nc), lambda b, t: (b * nt + t, 0))
                      for w, nc, cb in cast_weights]
    return pl.pallas_call(
        kern,
        grid=(batch, nt),
        in_specs=[
            pl.BlockSpec((tt, D_MODEL), lambda b, t: (b * nt + t, 0)),
            _mod_spec(mod9, 1, True),
            pl.BlockSpec((D_MODEL, BRANCH_COLS), lambda b, t: (0, 0)),
            pl.BlockSpec((CONV_W, D_RNN), lambda b, t: (0, 0)),
            vec,
            pl.BlockSpec((D_RNN // V7X_MXU_DIM, V7X_MXU_DIM, 2 * V7X_MXU_DIM), lambda b, t: (0, 0, 0)),
            vec, vec, vec, vec, vec,
            pl.BlockSpec((GMLP_GROUPS, CHUNK, CHUNK), lambda b, t: (0, 0, 0)),
            pl.BlockSpec((CHUNK, D_GMLP), lambda b, t: (0, 0)),
        ] + cast_in_specs,
        out_specs=[
            pl.BlockSpec((tt, D_RNN), lambda b, t: (b * nt + t, 0)),
            pl.BlockSpec((tt, D_GMLP), lambda b, t: (b * nt + t, 0)),
            pl.BlockSpec((1, 1, D_RNN), lambda b, t: (b, 0, 0)),
            pl.BlockSpec((1, V7X_SUBLANES, D_RNN), lambda b, t: (b, 0, 0)),
        ] + cast_out_specs,
        out_shape=[
            jax.ShapeDtypeStruct((batch * seq, D_RNN), BF16),
            jax.ShapeDtypeStruct((batch * seq, D_GMLP), BF16),
            jax.ShapeDtypeStruct((batch, 1, D_RNN), F32),
            jax.ShapeDtypeStruct((batch, V7X_SUBLANES, D_RNN), F32),
        ] + [jax.ShapeDtypeStruct((w.shape[0], nc), BF16) for w, nc, cb in cast_weights],
        scratch_shapes=[pltpu.VMEM((V7X_SUBLANES, D_RNN), F32), pltpu.VMEM((1, D_RNN), F32),
                        pltpu.VMEM((tt, D_RNN), F32)],
        compiler_params=_params("arbitrary", "arbitrary"),
        name="prompt_mixer",
    )(x, mod9, w_branch, conv_w, conv_b, wbd, ba, bx, lam, gln_g, gln_b, ws, bs_tile,
      *[w for w, nc, cb in cast_weights])


_PROJ_TN = 1024


def _sample_proj_kernel(x_ref, mod_ref, w_ref, lng_ref, lnb_ref, o_ref, wb_ref, u_ref, *, slabs):
    j = pl.program_id(0)

    @pl.when(j == 0)
    def _():
        u_ref[...] = _modulated_bf16(x_ref, mod_ref, slabs, None)

    wb_ref[...] = w_ref[...].astype(BF16)
    r = _dot(u_ref[...], wb_ref[...])

    @pl.when(jnp.logical_or(j == 0, j == 2))
    def _():
        o_ref[...] = r

    @pl.when(j == 1)
    def _():
        o_ref[...] = _gelu_tanh(r)

    @pl.when(j == 3)
    def _():
        o_ref[...] = _layer_norm(r, lng_ref[...], lnb_ref[...])


def _sample_proj(x, mod9, w_in, gln_g, gln_b):
    rows = x.shape[0]
    kern = functools.partial(_sample_proj_kernel, slabs=_slabs(rows, mod9, False))
    w_spec = pl.BlockSpec((D_MODEL, _PROJ_TN), lambda j: (0, j))
    return pl.pallas_call(
        kern,
        grid=(BRANCH_COLS // _PROJ_TN,),
        in_specs=[
            pl.BlockSpec((rows, D_MODEL), lambda j: (0, 0)),
            _mod_spec(mod9, 1, False),
            w_spec,
            pl.BlockSpec((1, D_GMLP), lambda j: (0, 0)),
            pl.BlockSpec((1, D_GMLP), lambda j: (0, 0)),
        ],
        out_specs=[pl.BlockSpec((rows, _PROJ_TN), lambda j: (0, j)), w_spec],
        out_shape=[jax.ShapeDtypeStruct((rows, BRANCH_COLS), F32),
                   jax.ShapeDtypeStruct((D_MODEL, BRANCH_COLS), BF16)],
        scratch_shapes=[pltpu.VMEM((rows, D_MODEL), BF16)],
        compiler_params=_params("arbitrary"),
        name="sample_proj",
    )(x, mod9, w_in, gln_g, gln_b)


def _sample_mix_kernel(xr_ref, gg_ref, gu_ref, vn_ref, sconv_ref, h0_ref, cw_ref, cb_ref,
                       wbd_ref, ba_ref, bx_ref, lam_ref, wsx_ref, bsx_ref,
                       ya_ref, yb_ref, ht_ref, xc_ref, *, nb, nt):
    def slab(t):
        return pl.ds(t * nb, nb)

    xpad = [sconv_ref[k] for k in range(CONV_W - 1)] + [xr_ref[slab(t), :] for t in range(nt)]
    for t in range(nt):
        xc = cb_ref[...]
        for k in range(CONV_W):
            xc = xc + xpad[t + k] * cw_ref[k:k + 1, :]
        xc_ref[slab(t), :] = xc

    a, mult, gated = _lru_coeffs(xc_ref[...], wbd_ref, ba_ref[...], bx_ref[...], lam_ref[...])
    b = mult * gated
    h = h0_ref[...]
    for t in range(nt):
        lo, hi = t * nb, (t + 1) * nb
        h = a[lo:hi] * h + b[lo:hi]
        ya_ref[slab(t), :] = (h * gg_ref[slab(t), :]).astype(BF16)
    ht_ref[...] = h

    for t in range(nt):
        s = bsx_ref[t:t + 1, :]
        for sp in range(t + 1):
            s = s + wsx_ref[t * nt + sp:t * nt + sp + 1, :] * vn_ref[slab(sp), :]
        yb_ref[slab(t), :] = (gu_ref[slab(t), :] * s).astype(BF16)


def _sample_mix(proj, sconv_tm, h0, conv_w, conv_b, wbd, ba, bx, lam, wsx, bsx, *, nb, nt):
    rows = nb * nt
    kern = functools.partial(_sample_mix_kernel, nb=nb, nt=nt)
    vec = pl.BlockSpec((1, D_RNN), lambda i: (0, 0))

    def col(jb):
        return pl.BlockSpec((rows, D_RNN), lambda i: (0, jb))

    return pl.pallas_call(
        kern,
        grid=(1,),
        in_specs=[
            col(0), col(1), col(2), col(3),
            pl.BlockSpec((CONV_W - 1, nb, D_RNN), lambda i: (0, 0, 0)),
            pl.BlockSpec((nb, D_RNN), lambda i: (0, 0)),
            pl.BlockSpec((CONV_W, D_RNN), lambda i: (0, 0)),
            vec,
            pl.BlockSpec((D_RNN // V7X_MXU_DIM, V7X_MXU_DIM, 2 * V7X_MXU_DIM), lambda i: (0, 0, 0)),
            vec, vec, vec,
            pl.BlockSpec((nt * nt, D_GMLP), lambda i: (0, 0)),
            pl.BlockSpec((nt, D_GMLP), lambda i: (0, 0)),
        ],
        out_specs=[
            pl.BlockSpec((rows, D_RNN), lambda i: (0, 0)),
            pl.BlockSpec((rows, D_GMLP), lambda i: (0, 0)),
            pl.BlockSpec((nb, D_RNN), lambda i: (0, 0)),
        ],
        out_shape=[
            jax.ShapeDtypeStruct((rows, D_RNN), BF16),
            jax.ShapeDtypeStruct((rows, D_GMLP), BF16),
            jax.ShapeDtypeStruct((nb, D_RNN), F32),
        ],
        scratch_shapes=[pltpu.VMEM((rows, D_RNN), F32)],
        compiler_params=_params("arbitrary"),
        name="sample_mix",
    )(proj, proj, proj, proj, sconv_tm, h0, conv_w, conv_b, wbd, ba, bx, lam, wsx, bsx)


def _merge_kernel(ya_ref, yb_ref, x_ref, mod_ref, wgate_ref, wpa_ref, wpb_ref, wo_ref,
                  lng_ref, lnb_ref, o_ref, *, slabs, tiles_per_batch):
    row = None if tiles_per_batch is None else pl.program_id(0) // tiles_per_batch
    u = _modulated_bf16(x_ref, mod_ref, slabs, row)
    gate = _mod(mod_ref, 2, row)
    y_a = jax.nn.sigmoid(_dot(u, wgate_ref[:, :D_MODEL])) * _dot(ya_ref[...], wpa_ref[...])
    y_b = jax.nn.sigmoid(_dot(u, wgate_ref[:, D_MODEL:])) * _dot(yb_ref[...], wpb_ref[...])
    mix = _dot((y_a + y_b).astype(BF16), wo_ref[...])
    for r0, nr in slabs:
        y = ALPHA * x_ref[r0:r0 + nr, :] + gate * mix[r0:r0 + nr, :]
        o_ref[r0:r0 + nr, :] = _layer_norm(y, lng_ref[...], lnb_ref[...])


def _merge(ya, yb, x, mod9, prompt_seq, w_gate, w_pa, w_pb, w_out, ln_g, ln_b, *, tm):
    rows = x.shape[0]
    prompt = prompt_seq is not None
    kern = functools.partial(_merge_kernel, slabs=_slabs(tm, mod9, prompt),
                             tiles_per_batch=prompt_seq // tm if prompt else None)
    whole = lambda shape: pl.BlockSpec(shape, lambda i: (0, 0))
    in_specs = [
        pl.BlockSpec((tm, D_RNN), lambda i: (i, 0)),
        pl.BlockSpec((tm, D_GMLP), lambda i: (i, 0)),
        pl.BlockSpec((tm, D_MODEL), lambda i: (i, 0)),
        _mod_spec(mod9, 1, prompt),
        whole((D_MODEL, 2 * D_MODEL)),
        whole((D_RNN, D_MODEL)),
        whole((D_GMLP, D_MODEL)),
        whole((D_MODEL, D_MODEL)),
        whole((1, D_MODEL)),
        whole((1, D_MODEL)),
    ]
    return pl.pallas_call(
        kern,
        grid=(rows // tm,),
        in_specs=in_specs,
        out_specs=pl.BlockSpec((tm, D_MODEL), lambda i: (i, 0)),
        out_shape=jax.ShapeDtypeStruct((rows, D_MODEL), F32),
        compiler_params=_params("arbitrary", vmem_limit_bytes=V7X_VMEM_FFN_LIMIT_BYTES),
        name="merge_out",
    )(ya, yb, x, mod9, w_gate, w_pa, w_pb, w_out, ln_g, ln_b)


def _block_diag_tiles(w):
    per = V7X_MXU_DIM // RNN_BW
    w4 = w.reshape(RNN_BLOCKS // per, per, RNN_BW, RNN_BW)
    eye = jnp.eye(per, dtype=w.dtype)
    t = w4[:, :, :, None, :] * eye[None, :, None, :, None]
    return t.reshape(RNN_BLOCKS // per, V7X_MXU_DIM, V7X_MXU_DIM)


def kernel(x_prompt, x_sample, state_conv, state_h, c_prompt, c_sample, w_ada, b_ada, ffn1_w_gu, ffn1_w_down, ffn2_w_gu, ffn2_w_down, w_in, conv_w, conv_b, lru_wa, lru_ba, lru_wx, lru_bx, lru_lambda, gmlp_ln_g, gmlp_ln_b, gmlp_ws, gmlp_bs, w_pa, w_pb, w_out, ln_g, ln_b):
    assert w_ada.shape[0] == DEPTH == 1
    bp, seq, _ = x_prompt.shape
    bs, dseq, _ = x_sample.shape
    l = 0

    wbd = jnp.concatenate([_block_diag_tiles(lru_wa[l]), _block_diag_tiles(lru_wx[l])],
                          axis=-1).astype(BF16)

    row = lambda v: v.reshape(1, -1)
    cb, ba, bx, lam = row(conv_b[l]), row(lru_ba[l]), row(lru_bx[l]), row(lru_lambda[l])
    gln_g, gln_b = row(gmlp_ln_g[l]), row(gmlp_ln_b[l])
    lng = [row(ln_g[l, k]) for k in range(3)]
    lnb = [row(ln_b[l, k]) for k in range(3)]

    bs_tile = jnp.repeat(gmlp_bs[l].T, GMLP_GW, axis=1)
    wsx = jnp.repeat(jnp.transpose(gmlp_ws[l][:, :dseq, :dseq], (1, 2, 0)), GMLP_GW, axis=2)
    wsx = wsx.reshape(dseq * dseq, D_GMLP)
    bsx = bs_tile[:dseq]

    assert bs % V7X_SUBLANES == 0 and bp <= _PROMPT_MOD_ROWS
    c_all = jnp.concatenate(
        [c_sample, c_prompt, jnp.zeros((_PROMPT_MOD_ROWS - bp, D_MODEL), c_prompt.dtype)], axis=0)
    mod9 = _modulation(c_all, w_ada[l], row(b_ada[l]))

    xp = x_prompt.reshape(bp * seq, D_MODEL)
    xs = jnp.transpose(x_sample, (1, 0, 2)).reshape(dseq * bs, D_MODEL)
    n_s = dseq * bs
    keep = CONV_W - 1

    def ffn(x, prompt_seq, k, w_g, w_v, v_col0, w_down, tm, tf, emit_bf16=False, cast_weights=()):
        return _ffn(x, mod9, k, prompt_seq, w_g, w_v, v_col0, w_down, lng[k], lnb[k],
                    tm=tm, tf=tf, emit_bf16=emit_bf16, cast_weights=cast_weights)

    def merge(ya, yb, x1, prompt_seq, merge_w, tm):
        return _merge(ya, yb, x1, mod9, prompt_seq, *merge_w, lng[1], lnb[1], tm=tm)

    x1s, w_g_b, w_v_b, w_d_b = ffn(xs, None, 0, ffn1_w_gu[l], ffn1_w_gu[l], D_FF, ffn1_w_down[l],
                                   n_s, 512, emit_bf16=True)
    x1p, w_gu2_b, w_dn2_b = ffn(xp, seq, 0, w_g_b, w_v_b, 0, w_d_b, 1024, 512,
                                cast_weights=((ffn2_w_gu[l], 32), (ffn2_w_down[l], 64)))

    proj_s, w_branch = _sample_proj(x1s, mod9, w_in[l], gln_g, gln_b)
    sconv_tm = jnp.transpose(state_conv[l], (1, 0, 2))
    ya_s, yb_s, ht_s = _sample_mix(proj_s, sconv_tm, state_h[l], conv_w[l], cb, wbd, ba, bx, lam,
                                   wsx, bsx, nb=bs, nt=dseq)
    ya_p, yb_p, ht_p, tail_p, *merge_w = _prompt_mixer(
        x1p, mod9, w_branch, conv_w[l], cb, wbd, ba, bx, lam, gln_g, gln_b,
        gmlp_ws[l], bs_tile, batch=bp, seq=seq, tt=512,
        cast_weights=((w_in[l], 2 * D_MODEL, 1), (w_pa[l], D_MODEL, 0), (w_pb[l], D_MODEL, 0),
                      (w_out[l], D_MODEL, 0)))
    x2s = merge(ya_s, yb_s, x1s, None, merge_w, 256)
    x2p = merge(ya_p, yb_p, x1p, seq, merge_w, 512)

    (x3p,) = ffn(x2p, seq, 2, w_gu2_b, w_gu2_b, D_FF, w_dn2_b, 1024, 512)
    (x3s,) = ffn(x2s, None, 2, w_gu2_b, w_gu2_b, D_FF, w_dn2_b, n_s, 512)

    y_prompt = x3p.reshape(bp, seq, D_MODEL)
    y_sample = jnp.transpose(x3s.reshape(dseq, bs, D_MODEL), (1, 0, 2))
    new_conv_p = tail_p[:, V7X_SUBLANES - keep:][None]
    new_h_p = ht_p.reshape(bp, D_RNN)[None]
    xr_s = lax.slice(proj_s.reshape(dseq, bs, BRANCH_COLS), (dseq - keep, 0, 0), (dseq, bs, D_RNN))
    new_conv_s = jnp.transpose(xr_s, (1, 0, 2))[None]
    new_h_s = ht_s[None]
    vn_s = lax.slice(proj_s.reshape(dseq, bs, BRANCH_COLS), (0, 0, 2 * D_RNN + D_GMLP),
                     (dseq, bs, BRANCH_COLS))
    new_v_s = jnp.transpose(vn_s, (1, 0, 2))[None]
    return (y_prompt, y_sample, new_conv_p, new_h_p, new_conv_s, new_h_s, new_v_s)
```

```python
import functools
import math

import jax
import jax.numpy as jnp
from jax import lax
from jax.experimental import pallas as pl
from jax.experimental.pallas import tpu as pltpu

D_MODEL = 2048
D_RNN = 1024
RNN_BLOCKS = 16
RNN_BW = D_RNN // RNN_BLOCKS
CONV_W = 4
LRU_C = 8.0
D_GMLP = 1024
CHUNK = 128
GMLP_GROUPS = 8
GMLP_GW = D_GMLP // GMLP_GROUPS
D_FF = 5632
N_MOD = 9
IN_COLS = 2 * D_RNN + 2 * D_GMLP + 2 * D_MODEL
BRANCH_COLS = 2 * D_RNN + 2 * D_GMLP
LN_EPS = 1e-5
DEPTH = 1
ALPHA = (2.0 * DEPTH) ** 0.25

V7X_SUBLANES = 8
V7X_LANES = 128
V7X_MXU_DIM = 256
V7X_VMEM_BYTES = 64 * 1024 * 1024
V7X_VMEM_LIMIT_BYTES = 58 * 1024 * 1024
V7X_VMEM_FFN_LIMIT_BYTES = V7X_VMEM_BYTES - 512 * 1024

BF16 = jnp.bfloat16
F32 = jnp.float32


def _dot(a, b):
    return jnp.dot(a, b, preferred_element_type=F32)


def _layer_norm(x, g, b):
    mu = jnp.mean(x, axis=-1, keepdims=True)
    xc = x - mu
    var = jnp.mean(xc * xc, axis=-1, keepdims=True)
    return xc * lax.rsqrt(var + LN_EPS) * g + b


def _gelu_tanh(x):
    c = math.sqrt(2.0 / math.pi)
    return 0.5 * x * (1.0 + jnp.tanh(c * (x + 0.044715 * (x * x * x))))


def _softplus(x):
    return jnp.maximum(x, 0.0) + jnp.log1p(jnp.exp(-jnp.abs(x)))


def _params(*semantics, vmem_limit_bytes=V7X_VMEM_LIMIT_BYTES):
    return pltpu.CompilerParams(dimension_semantics=semantics, vmem_limit_bytes=vmem_limit_bytes)


_PROMPT_MOD_ROWS = V7X_SUBLANES


def _mod_spec(mod9, sub_layer, prompt):
    n_sample = mod9.shape[1] - _PROMPT_MOD_ROWS
    if prompt:
        return pl.BlockSpec((3, _PROMPT_MOD_ROWS, D_MODEL),
                            lambda i, *_: (sub_layer, n_sample // _PROMPT_MOD_ROWS, 0))
    return pl.BlockSpec((3, n_sample, D_MODEL), lambda i, *_: (sub_layer, 0, 0))


def _slabs(tm, mod9, prompt):
    n_sample = mod9.shape[1] - _PROMPT_MOD_ROWS
    if prompt:
        return [(0, tm)]
    return [(s, n_sample) for s in range(0, tm, n_sample)]


def _mod(mod_ref, c, row):
    return mod_ref[c] if row is None else mod_ref[c, pl.ds(row, 1), :]


def _modulated_bf16(x_ref, mod_ref, slabs, row):
    scale1, shift = 1.0 + _mod(mod_ref, 1, row), _mod(mod_ref, 0, row)
    parts = [(x_ref[r0:r0 + nr, :] * scale1 + shift).astype(BF16) for r0, nr in slabs]
    return parts[0] if len(parts) == 1 else jnp.concatenate(parts, axis=0)


def _mod_kernel(c_ref, w_ref, b_ref, o_ref):
    c = c_ref[...]
    a = (c * jax.nn.sigmoid(c)).astype(BF16)
    o_ref[0] = _dot(a, w_ref[...].astype(BF16)) + b_ref[...]


def _modulation(c_all, w_ada, b_ada):
    rows = c_all.shape[0]
    return pl.pallas_call(
        _mod_kernel,
        grid=(N_MOD,),
        in_specs=[
            pl.BlockSpec((rows, D_MODEL), lambda j: (0, 0)),
            pl.BlockSpec((D_MODEL, D_MODEL), lambda j: (0, j)),
            pl.BlockSpec((1, D_MODEL), lambda j: (0, j)),
        ],
        out_specs=pl.BlockSpec((1, rows, D_MODEL), lambda j: (j, 0, 0)),
        out_shape=jax.ShapeDtypeStruct((N_MOD, rows, D_MODEL), F32),
        compiler_params=_params("arbitrary"),
        name="adaln_mod",
    )(c_all, w_ada, b_ada)


def _ffn_kernel(x_ref, mod_ref, wg_ref, wv_ref, wd_ref, lng_ref, lnb_ref, *rest,
                slabs, n_f, emit_bf16, tiles_per_batch, n_cast, row_tile0, has_base):
    cast_in = rest[:n_cast]
    n_in = n_cast + (1 if has_base else 0)
    o_ref = rest[n_in]
    emit_out = rest[n_in + 1:n_in + 4] if emit_bf16 else ()
    cast_out = rest[n_in + 1 + len(emit_out):-1]
    u_ref = rest[-1]
    j = pl.program_id(1)
    row = (None if tiles_per_batch is None
           else (pl.program_id(0) + row_tile0) // tiles_per_batch)

    def swiglu_down():
        for src, dst in zip(cast_in, cast_out):
            dst[...] = src[...].astype(BF16)
        if emit_bf16:
            wg, wv, wd = (r[...].astype(BF16) for r in (wg_ref, wv_ref, wd_ref))
            for dst, w in zip(emit_out, (wg, wv, wd)):
                dst[...] = w
        else:
            wg, wv, wd = wg_ref[...], wv_ref[...], wd_ref[...]
        u = u_ref[...]
        g = _dot(u, wg)
        v = _dot(u, wv)
        h = (g * jax.nn.sigmoid(g) * v).astype(BF16)
        return _dot(h, wd)

    @pl.when(j == 0)
    def _():
        u_ref[...] = _modulated_bf16(x_ref, mod_ref, slabs, row)
        o_ref[...] = swiglu_down()

    @pl.when(jnp.logical_and(j > 0, j < n_f - 1))
    def _():
        o_ref[...] += swiglu_down()

    @pl.when(j == n_f - 1)
    def _():
        o_ref[...] += swiglu_down()
        for r0, nr in slabs:
            rows = pl.ds(r0, nr)
            y = ALPHA * x_ref[rows, :] + 0.5 * _mod(mod_ref, 2, row) * o_ref[rows, :]
            o_ref[rows, :] = _layer_norm(y, lng_ref[...], lnb_ref[...])


def _ffn(x, mod9, sub_layer, prompt_seq, w_g, w_v, v_col0, w_down, ln_g, ln_b, *, tm, tf,
         emit_bf16, cast_weights=(), row_tiles=None, base=None):
    rows = x.shape[0]
    prompt = prompt_seq is not None
    n_f = D_FF // tf
    v_blk0 = v_col0 // tf
    tile0, n_tiles = row_tiles if row_tiles is not None else (0, rows // tm)
    steps = n_tiles * n_f
    kern = functools.partial(_ffn_kernel, slabs=_slabs(tm, mod9, prompt), n_f=n_f,
                             emit_bf16=emit_bf16,
                             tiles_per_batch=prompt_seq // tm if prompt else None,
                             n_cast=len(cast_weights), row_tile0=tile0,
                             has_base=base is not None)
    up_spec = pl.BlockSpec((D_MODEL, tf), lambda i, j: (0, j))
    down_spec = pl.BlockSpec((tf, D_MODEL), lambda i, j: (j, 0))
    row_spec = pl.BlockSpec((tm, D_MODEL), lambda i, j: (i + tile0, 0))
    out_specs = [row_spec]
    out_shape = [jax.ShapeDtypeStruct((rows, D_MODEL), F32)]
    cast_specs = []
    for w, slab_rows in cast_weights:
        n_slabs = w.shape[0] // slab_rows
        assert n_slabs * slab_rows == w.shape[0] and n_slabs <= steps
        cast_specs.append(pl.BlockSpec(
            (slab_rows, w.shape[1]),
            lambda i, j, n_slabs=n_slabs: (jnp.minimum(i * n_f + j, n_slabs - 1), 0)))
    if emit_bf16:
        pad = tf if n_tiles > 1 else 0
        emit_blk = lambda i, j: jnp.where(i == 0, j, n_f)
        out_specs += [pl.BlockSpec((D_MODEL, tf), lambda i, j: (0, emit_blk(i, j)))] * 2
        out_specs += [pl.BlockSpec((tf, D_MODEL), lambda i, j: (emit_blk(i, j), 0))]
        out_shape += [jax.ShapeDtypeStruct((D_MODEL, D_FF + pad), BF16)] * 2
        out_shape += [jax.ShapeDtypeStruct((D_FF + pad, D_MODEL), BF16)]
    out_specs += cast_specs
    out_shape += [jax.ShapeDtypeStruct(w.shape, BF16) for w, _ in cast_weights]
    in_specs = [
        row_spec,
        _mod_spec(mod9, sub_layer, prompt),
        up_spec,
        pl.BlockSpec((D_MODEL, tf), lambda i, j: (0, v_blk0 + j)),
        down_spec,
        pl.BlockSpec((1, D_MODEL), lambda i, j: (0, 0)),
        pl.BlockSpec((1, D_MODEL), lambda i, j: (0, 0)),
    ] + cast_specs
    args = [x, mod9, w_g, w_v, w_down, ln_g, ln_b] + [w for w, _ in cast_weights]
    aliases = {}
    if base is not None:
        in_specs.append(pl.BlockSpec(memory_space=pl.ANY))
        args.append(base)
        aliases = {len(args) - 1: 0}
    return pl.pallas_call(
        kern,
        grid=(n_tiles, n_f),
        in_specs=in_specs,
        out_specs=out_specs,
        out_shape=out_shape,
        input_output_aliases=aliases,
        scratch_shapes=[pltpu.VMEM((tm, D_MODEL), BF16)],
        compiler_params=_params("arbitrary", "arbitrary", vmem_limit_bytes=V7X_VMEM_FFN_LIMIT_BYTES),
        name="macaron_ffn",
    )(*args)


def _lru_coeffs(xc, wbd_ref, ba, bx, lam):
    xcb = xc.astype(BF16)
    ra, rx = [], []
    for q in range(D_RNN // V7X_MXU_DIM):
        cols = slice(q * V7X_MXU_DIM, (q + 1) * V7X_MXU_DIM)
        rq = _dot(xcb[:, cols], wbd_ref[q])
        ra.append(rq[:, :V7X_MXU_DIM])
        rx.append(rq[:, V7X_MXU_DIM:])
    r = jax.nn.sigmoid(jnp.concatenate(ra, axis=-1) + ba)
    i = jax.nn.sigmoid(jnp.concatenate(rx, axis=-1) + bx)
    log_a = (-LRU_C * r) * _softplus(-lam)
    a = jnp.exp(log_a)
    z = -jnp.tanh(log_a) * (a * a + 1.0)
    mult = jnp.where(z > 0.0, z * lax.rsqrt(z), 0.0)
    return a, mult, i * xc


_MIXER_INPUTS = 13


def _prompt_mixer_kernel(*refs, tt, n_cast):
    (x_ref, mod_ref, w_ref, cw_ref, cb_ref, wbd_ref, ba_ref, bx_ref, lam_ref,
     glng_ref, glnb_ref, ws_ref, bs_ref) = refs[:_MIXER_INPUTS]
    cast_in = refs[_MIXER_INPUTS:_MIXER_INPUTS + n_cast]
    ya_ref, yb_ref, ht_ref, tailo_ref = refs[_MIXER_INPUTS + n_cast:_MIXER_INPUTS + n_cast + 4]
    cast_out = refs[_MIXER_INPUTS + n_cast + 4:_MIXER_INPUTS + 2 * n_cast + 4]
    tail_ref, h_ref, hs_ref = refs[_MIXER_INPUTS + 2 * n_cast + 4:]
    t = pl.program_id(1)
    ng = tt // V7X_SUBLANES

    for src, dst in zip(cast_in, cast_out):
        dst[...] = src[...].astype(BF16)

    @pl.when(t == 0)
    def _():
        tail_ref[...] = jnp.zeros_like(tail_ref)
        h_ref[...] = jnp.zeros_like(h_ref)

    u = _modulated_bf16(x_ref, mod_ref, [(0, tt)], pl.program_id(0))

    xr = _dot(u, w_ref[:, 0:D_RNN])
    gr = _dot(u, w_ref[:, D_RNN:2 * D_RNN])
    gu = _dot(u, w_ref[:, 2 * D_RNN:2 * D_RNN + D_GMLP])
    gv = _dot(u, w_ref[:, 2 * D_RNN + D_GMLP:BRANCH_COLS])

    prev = tail_ref[...]
    row = lax.broadcasted_iota(jnp.int32, (tt, D_RNN), 0)
    row8 = lax.broadcasted_iota(jnp.int32, (V7X_SUBLANES, D_RNN), 0)
    xc = cb_ref[...]
    for k in range(CONV_W):
        s = CONV_W - 1 - k
        if s == 0:
            xs = xr
        else:
            rolled = pltpu.roll(xr, s, 0)
            head = jnp.where(row8 < s, pltpu.roll(prev, s, 0), rolled[:V7X_SUBLANES])
            xs = jnp.concatenate([head, rolled[V7X_SUBLANES:]], axis=0)
        xc = xc + xs * cw_ref[k:k + 1, :]
    new_tail = xr[tt - V7X_SUBLANES:, :]
    tail_ref[...] = new_tail
    tailo_ref[0] = new_tail

    a, mult, gated = _lru_coeffs(xc, wbd_ref, ba_ref[...], bx_ref[...], lam_ref[...])
    mult = jnp.where(jnp.logical_and(row == 0, t == 0), 1.0, mult)
    b = mult * gated

    a = a.reshape(ng, V7X_SUBLANES, D_RNN)
    b = b.reshape(ng, V7X_SUBLANES, D_RNN)
    sub = lax.broadcasted_iota(jnp.int32, (ng, V7X_SUBLANES, D_RNN), 1)
    s = 1
    while s < V7X_SUBLANES:
        keep = sub >= s
        a_prev = jnp.where(keep, pltpu.roll(a, s, 1), 1.0)
        b_prev = jnp.where(keep, pltpu.roll(b, s, 1), 0.0)
        b = a * b_prev + b
        a = a * a_prev
        s *= 2
    h_in = h_ref[...]
    for g in range(ng):
        h = a[g] * h_in + b[g]
        hs_ref[g * V7X_SUBLANES:(g + 1) * V7X_SUBLANES, :] = h
        h_in = h[V7X_SUBLANES - 1:, :]
    h_ref[...] = h_in
    ht_ref[0] = h_in
    ya_ref[...] = (hs_ref[...] * _gelu_tanh(gr)).astype(BF16)

    vn = _layer_norm(gv, glng_ref[...], glnb_ref[...]).astype(BF16)
    ti = lax.broadcasted_iota(jnp.int32, (CHUNK, CHUNK), 0)
    si = lax.broadcasted_iota(jnp.int32, (CHUNK, CHUNK), 1)
    causal = si <= ti
    for g in range(GMLP_GROUPS):
        w = jnp.where(causal, ws_ref[g], 0.0).astype(BF16)
        c0 = g * GMLP_GW
        bias = bs_ref[:, c0:c0 + GMLP_GW]
        n_chunks = tt // CHUNK
        rhs = jnp.concatenate([vn[c * CHUNK:(c + 1) * CHUNK, c0:c0 + GMLP_GW]
                               for c in range(n_chunks)], axis=1)
        s_all = _dot(w, rhs)
        for c in range(n_chunks):
            r0 = c * CHUNK
            s_blk = s_all[:, c * GMLP_GW:(c + 1) * GMLP_GW] + bias
            yb_ref[r0:r0 + CHUNK, c0:c0 + GMLP_GW] = (
                gu[r0:r0 + CHUNK, c0:c0 + GMLP_GW] * s_blk).astype(BF16)


def _prompt_mixer(x, mod9, w_branch, conv_w, conv_b, wbd, ba, bx, lam, gln_g, gln_b, ws, bs_tile,
                  *, batch, seq, tt, cast_weights=()):
    nt = seq // tt
    steps = batch * nt
    kern = functools.partial(_prompt_mixer_kernel, tt=tt, n_cast=len(cast_weights))
    vec = pl.BlockSpec((1, D_RNN), lambda b, t: (0, 0))
    cast_in_specs = [pl.BlockSpec((w.shape[0] // steps, nc), lambda b, t, cb=cb: (b * nt + t, cb))
                     for w, nc, cb in cast_weights]
    cast_out_specs = [pl.BlockSpec((w.shape[0] // steps, nc), lambda b, t: (b * nt + t, 0))
                      for w, nc, cb in cast_weights]
    return pl.pallas_call(
        kern,
        grid=(batch, nt),
        in_specs=[
            pl.BlockSpec((tt, D_MODEL), lambda b, t: (b * nt + t, 0)),
            _mod_spec(mod9, 1, True),
            pl.BlockSpec((D_MODEL, BRANCH_COLS), lambda b, t: (0, 0)),
            pl.BlockSpec((CONV_W, D_RNN), lambda b, t: (0, 0)),
            vec,
            pl.BlockSpec((D_RNN // V7X_MXU_DIM, V7X_MXU_DIM, 2 * V7X_MXU_DIM), lambda b, t: (0, 0, 0)),
            vec, vec, vec, vec, vec,
            pl.BlockSpec((GMLP_GROUPS, CHUNK, CHUNK), lambda b, t: (0, 0, 0)),
            pl.BlockSpec((CHUNK, D_GMLP), lambda b, t: (0, 0)),
        ] + cast_in_specs,
        out_specs=[
            pl.BlockSpec((tt, D_RNN), lambda b, t: (b * nt + t, 0)),
            pl.BlockSpec((tt, D_GMLP), lambda b, t: (b * nt + t, 0)),
            pl.BlockSpec((1, 1, D_RNN), lambda b, t: (b, 0, 0)),
            pl.BlockSpec((1, V7X_SUBLANES, D_RNN), lambda b, t: (b, 0, 0)),
        ] + cast_out_specs,
        out_shape=[
            jax.ShapeDtypeStruct((batch * seq, D_RNN), BF16),
            jax.ShapeDtypeStruct((batch * seq, D_GMLP), BF16),
            jax.ShapeDtypeStruct((batch, 1, D_RNN), F32),
            jax.ShapeDtypeStruct((batch, V7X_SUBLANES, D_RNN), F32),
        ] + [jax.ShapeDtypeStruct((w.shape[0], nc), BF16) for w, nc, cb in cast_weights],
        scratch_shapes=[pltpu.VMEM((V7X_SUBLANES, D_RNN), F32), pltpu.VMEM((1, D_RNN), F32),
                        pltpu.VMEM((tt, D_RNN), F32)],
        compiler_params=_params("arbitrary", "arbitrary"),
        name="prompt_mixer",
    )(x, mod9, w_branch, conv_w, conv_b, wbd, ba, bx, lam, gln_g, gln_b, ws, bs_tile,
      *[w for w, nc, cb in cast_weights])


_PROJ_TN = 1024


def _sample_proj_kernel(x_ref, mod_ref, w_ref, lng_ref, lnb_ref, o_ref, wb_ref, u_ref, *, slabs):
    j = pl.program_id(0)

    @pl.when(j == 0)
    def _():
        u_ref[...] = _modulated_bf16(x_ref, mod_ref, slabs, None)

    wb_ref[...] = w_ref[...].astype(BF16)
    r = _dot(u_ref[...], wb_ref[...])

    @pl.when(jnp.logical_or(j == 0, j == 2))
    def _():
        o_ref[...] = r

    @pl.when(j == 1)
    def _():
        o_ref[...] = _gelu_tanh(r)

    @pl.when(j == 3)
    def _():
        o_ref[...] = _layer_norm(r, lng_ref[...], lnb_ref[...])


def _sample_proj(x, mod9, w_in, gln_g, gln_b):
    rows = x.shape[0]
    kern = functools.partial(_sample_proj_kernel, slabs=_slabs(rows, mod9, False))
    w_spec = pl.BlockSpec((D_MODEL, _PROJ_TN), lambda j: (0, j))
    return pl.pallas_call(
        kern,
        grid=(BRANCH_COLS // _PROJ_TN,),
        in_specs=[
            pl.BlockSpec((rows, D_MODEL), lambda j: (0, 0)),
            _mod_spec(mod9, 1, False),
            w_spec,
            pl.BlockSpec((1, D_GMLP), lambda j: (0, 0)),
            pl.BlockSpec((1, D_GMLP), lambda j: (0, 0)),
        ],
        out_specs=[pl.BlockSpec((rows, _PROJ_TN), lambda j: (0, j)), w_spec],
        out_shape=[jax.ShapeDtypeStruct((rows, BRANCH_COLS), F32),
                   jax.ShapeDtypeStruct((D_MODEL, BRANCH_COLS), BF16)],
        scratch_shapes=[pltpu.VMEM((rows, D_MODEL), BF16)],
        compiler_params=_params("arbitrary"),
        name="sample_proj",
    )(x, mod9, w_in, gln_g, gln_b)


def _sample_mix_kernel(xr_ref, gg_ref, gu_ref, vn_ref, sconv_ref, h0_ref, cw_ref, cb_ref,
                       wbd_ref, ba_ref, bx_ref, lam_ref, wsx_ref, bsx_ref,
                       ya_ref, yb_ref, ht_ref, xc_ref, *, nb, nt):
    def slab(t):
        return pl.ds(t * nb, nb)

    xpad = [sconv_ref[k] for k in range(CONV_W - 1)] + [xr_ref[slab(t), :] for t in range(nt)]
    for t in range(nt):
        xc = cb_ref[...]
        for k in range(CONV_W):
            xc = xc + xpad[t + k] * cw_ref[k:k + 1, :]
        xc_ref[slab(t), :] = xc

    a, mult, gated = _lru_coeffs(xc_ref[...], wbd_ref, ba_ref[...], bx_ref[...], lam_ref[...])
    b = mult * gated
    h = h0_ref[...]
    for t in range(nt):
        lo, hi = t * nb, (t + 1) * nb
        h = a[lo:hi] * h + b[lo:hi]
        ya_ref[slab(t), :] = (h * gg_ref[slab(t), :]).astype(BF16)
    ht_ref[...] = h

    for t in range(nt):
        s = bsx_ref[t:t + 1, :]
        for sp in range(t + 1):
            s = s + wsx_ref[t * nt + sp:t * nt + sp + 1, :] * vn_ref[slab(sp), :]
        yb_ref[slab(t), :] = (gu_ref[slab(t), :] * s).astype(BF16)


def _sample_mix(proj, sconv_tm, h0, conv_w, conv_b, wbd, ba, bx, lam, wsx, bsx, *, nb, nt):
    rows = nb * nt
    kern = functools.partial(_sample_mix_kernel, nb=nb, nt=nt)
    vec = pl.BlockSpec((1, D_RNN), lambda i: (0, 0))

    def col(jb):
        return pl.BlockSpec((rows, D_RNN), lambda i: (0, jb))

    return pl.pallas_call(
        kern,
        grid=(1,),
        in_specs=[
            col(0), col(1), col(2), col(3),
            pl.BlockSpec((CONV_W - 1, nb, D_RNN), lambda i: (0, 0, 0)),
            pl.BlockSpec((nb, D_RNN), lambda i: (0, 0)),
            pl.BlockSpec((CONV_W, D_RNN), lambda i: (0, 0)),
            vec,
            pl.BlockSpec((D_RNN // V7X_MXU_DIM, V7X_MXU_DIM, 2 * V7X_MXU_DIM), lambda i: (0, 0, 0)),
            vec, vec, vec,
            pl.BlockSpec((nt * nt, D_GMLP), lambda i: (0, 0)),
            pl.BlockSpec((nt, D_GMLP), lambda i: (0, 0)),
        ],
        out_specs=[
            pl.BlockSpec((rows, D_RNN), lambda i: (0, 0)),
            pl.BlockSpec((rows, D_GMLP), lambda i: (0, 0)),
            pl.BlockSpec((nb, D_RNN), lambda i: (0, 0)),
        ],
        out_shape=[
            jax.ShapeDtypeStruct((rows, D_RNN), BF16),
            jax.ShapeDtypeStruct((rows, D_GMLP), BF16),
            jax.ShapeDtypeStruct((nb, D_RNN), F32),
        ],
        scratch_shapes=[pltpu.VMEM((rows, D_RNN), F32)],
        compiler_params=_params("arbitrary"),
        name="sample_mix",
    )(proj, proj, proj, proj, sconv_tm, h0, conv_w, conv_b, wbd, ba, bx, lam, wsx, bsx)


def _merge_kernel(ya_ref, yb_ref, x_ref, mod_ref, wgate_ref, wpa_ref, wpb_ref, wo_ref,
                  lng_ref, lnb_ref, o_ref, *, slabs, tiles_per_batch):
    row = None if tiles_per_batch is None else pl.program_id(0) // tiles_per_batch
    u = _modulated_bf16(x_ref, mod_ref, slabs, row)
    gate = _mod(mod_ref, 2, row)
    y_a = jax.nn.sigmoid(_dot(u, wgate_ref[:, :D_MODEL])) * _dot(ya_ref[...], wpa_ref[...])
    y_b = jax.nn.sigmoid(_dot(u, wgate_ref[:, D_MODEL:])) * _dot(yb_ref[...], wpb_ref[...])
    mix = _dot((y_a + y_b).astype(BF16), wo_ref[...])
    for r0, nr in slabs:
        y = ALPHA * x_ref[r0:r0 + nr, :] + gate * mix[r0:r0 + nr, :]
        o_ref[r0:r0 + nr, :] = _layer_norm(y, lng_ref[...], lnb_ref[...])


def _merge(ya, yb, x, mod9, prompt_seq, w_gate, w_pa, w_pb, w_out, ln_g, ln_b, *, tm):
    rows = x.shape[0]
    prompt = prompt_seq is not None
    kern = functools.partial(_merge_kernel, slabs=_slabs(tm, mod9, prompt),
                             tiles_per_batch=prompt_seq // tm if prompt else None)
    whole = lambda shape: pl.BlockSpec(shape, lambda i: (0, 0))
    in_specs = [
        pl.BlockSpec((tm, D_RNN), lambda i: (i, 0)),
        pl.BlockSpec((tm, D_GMLP), lambda i: (i, 0)),
        pl.BlockSpec((tm, D_MODEL), lambda i: (i, 0)),
        _mod_spec(mod9, 1, prompt),
        whole((D_MODEL, 2 * D_MODEL)),
        whole((D_RNN, D_MODEL)),
        whole((D_GMLP, D_MODEL)),
        whole((D_MODEL, D_MODEL)),
        whole((1, D_MODEL)),
        whole((1, D_MODEL)),
    ]
    return pl.pallas_call(
        kern,
        grid=(rows // tm,),
        in_specs=in_specs,
        out_specs=pl.BlockSpec((tm, D_MODEL), lambda i: (i, 0)),
        out_shape=jax.ShapeDtypeStruct((rows, D_MODEL), F32),
        compiler_params=_params("arbitrary", vmem_limit_bytes=V7X_VMEM_FFN_LIMIT_BYTES),
        name="merge_out",
    )(ya, yb, x, mod9, w_gate, w_pa, w_pb, w_out, ln_g, ln_b)


def _block_diag_tiles(w):
    per = V7X_MXU_DIM // RNN_BW
    w4 = w.reshape(RNN_BLOCKS // per, per, RNN_BW, RNN_BW)
    eye = jnp.eye(per, dtype=w.dtype)
    t = w4[:, :, :, None, :] * eye[None, :, None, :, None]
    return t.reshape(RNN_BLOCKS // per, V7X_MXU_DIM, V7X_MXU_DIM)


def kernel(x_prompt, x_sample, state_conv, state_h, c_prompt, c_sample, w_ada, b_ada, ffn1_w_gu, ffn1_w_down, ffn2_w_gu, ffn2_w_down, w_in, conv_w, conv_b, lru_wa, lru_ba, lru_wx, lru_bx, lru_lambda, gmlp_ln_g, gmlp_ln_b, gmlp_ws, gmlp_bs, w_pa, w_pb, w_out, ln_g, ln_b):
    assert w_ada.shape[0] == DEPTH == 1
    bp, seq, _ = x_prompt.shape
    bs, dseq, _ = x_sample.shape
    l = 0

    wbd = jnp.concatenate([_block_diag_tiles(lru_wa[l]), _block_diag_tiles(lru_wx[l])],
                          axis=-1).astype(BF16)

    row = lambda v: v.reshape(1, -1)
    cb, ba, bx, lam = row(conv_b[l]), row(lru_ba[l]), row(lru_bx[l]), row(lru_lambda[l])
    gln_g, gln_b = row(gmlp_ln_g[l]), row(gmlp_ln_b[l])
    lng = [row(ln_g[l, k]) for k in range(3)]
    lnb = [row(ln_b[l, k]) for k in range(3)]

    bs_tile = jnp.repeat(gmlp_bs[l].T, GMLP_GW, axis=1)
    wsx = jnp.repeat(jnp.transpose(gmlp_ws[l][:, :dseq, :dseq], (1, 2, 0)), GMLP_GW, axis=2)
    wsx = wsx.reshape(dseq * dseq, D_GMLP)
    bsx = bs_tile[:dseq]

    assert bs % V7X_SUBLANES == 0 and bp <= _PROMPT_MOD_ROWS
    c_all = jnp.concatenate(
        [c_sample, c_prompt, jnp.zeros((_PROMPT_MOD_ROWS - bp, D_MODEL), c_prompt.dtype)], axis=0)
    mod9 = _modulation(c_all, w_ada[l], row(b_ada[l]))

    xp = x_prompt.reshape(bp * seq, D_MODEL)
    xs = jnp.transpose(x_sample, (1, 0, 2)).reshape(dseq * bs, D_MODEL)
    n_s = dseq * bs
    keep = CONV_W - 1

    def ffn(x, prompt_seq, k, w_g, w_v, v_col0, w_down, tm, tf, **kw):
        kw.setdefault("emit_bf16", False)
        return _ffn(x, mod9, k, prompt_seq, w_g, w_v, v_col0, w_down, lng[k], lnb[k],
                    tm=tm, tf=tf, **kw)

    def merge(ya, yb, x1, prompt_seq, merge_w, tm):
        return _merge(ya, yb, x1, mod9, prompt_seq, *merge_w, lng[1], lnb[1], tm=tm)

    x1p_head, w_g_b, w_v_b, w_d_b = ffn(xp, seq, 0, ffn1_w_gu[l], ffn1_w_gu[l], D_FF,
                                        ffn1_w_down[l], 1024, 256, emit_bf16=True,
                                        row_tiles=(0, 1))
    x1p, w_gu2_b, w_dn2_b = ffn(xp, seq, 0, w_g_b, w_v_b, 0, w_d_b, 1024, 512,
                                cast_weights=((ffn2_w_gu[l], 32), (ffn2_w_down[l], 128)),
                                row_tiles=(1, bp * seq // 1024 - 1), base=x1p_head)
    (x1s,) = ffn(xs, None, 0, w_g_b, w_v_b, 0, w_d_b, n_s, 512)

    proj_s, w_branch = _sample_proj(x1s, mod9, w_in[l], gln_g, gln_b)
    sconv_tm = jnp.transpose(state_conv[l], (1, 0, 2))
    ya_s, yb_s, ht_s = _sample_mix(proj_s, sconv_tm, state_h[l], conv_w[l], cb, wbd, ba, bx, lam,
                                   wsx, bsx, nb=bs, nt=dseq)
    ya_p, yb_p, ht_p, tail_p, *merge_w = _prompt_mixer(
        x1p, mod9, w_branch, conv_w[l], cb, wbd, ba, bx, lam, gln_g, gln_b,
        gmlp_ws[l], bs_tile, batch=bp, seq=seq, tt=512,
        cast_weights=((w_in[l], 2 * D_MODEL, 1), (w_pa[l], D_MODEL, 0), (w_pb[l], D_MODEL, 0),
                      (w_out[l], D_MODEL, 0)))
    x2s = merge(ya_s, yb_s, x1s, None, merge_w, 256)
    x2p = merge(ya_p, yb_p, x1p, seq, merge_w, 512)

    (x3p,) = ffn(x2p, seq, 2, w_gu2_b, w_gu2_b, D_FF, w_dn2_b, 1024, 512)
    (x3s,) = ffn(x2s, None, 2, w_gu2_b, w_gu2_b, D_FF, w_dn2_b, n_s, 512)

    y_prompt = x3p.reshape(bp, seq, D_MODEL)
    y_sample = jnp.transpose(x3s.reshape(dseq, bs, D_MODEL), (1, 0, 2))
    new_conv_p = tail_p[:, V7X_SUBLANES - keep:][None]
    new_h_p = ht_p.reshape(bp, D_RNN)[None]
    xr_s = lax.slice(proj_s.reshape(dseq, bs, BRANCH_COLS), (dseq - keep, 0, 0), (dseq, bs, D_RNN))
    new_conv_s = jnp.transpose(xr_s, (1, 0, 2))[None]
    new_h_s = ht_s[None]
    vn_s = lax.slice(proj_s.reshape(dseq, bs, BRANCH_COLS), (0, 0, 2 * D_RNN + D_GMLP),
                     (dseq, bs, BRANCH_COLS))
    new_v_s = jnp.transpose(vn_s, (1, 0, 2))[None]
    return (y_prompt, y_sample, new_conv_p, new_h_p, new_conv_s, new_h_s, new_v_s)
```

```python
import functools
import math

import jax
import jax.numpy as jnp
from jax import lax
from jax.experimental import pallas as pl
from jax.experimental.pallas import tpu as pltpu

D_MODEL = 2048
D_RNN = 1024
RNN_BLOCKS = 16
RNN_BW = D_RNN // RNN_BLOCKS
CONV_W = 4
LRU_C = 8.0
D_GMLP = 1024
CHUNK = 128
GMLP_GROUPS = 8
GMLP_GW = D_GMLP // GMLP_GROUPS
D_FF = 5632
N_MOD = 9
BRANCH_COLS = 2 * D_RNN + 2 * D_GMLP
LN_EPS = 1e-5
DEPTH = 1
ALPHA = (2.0 * DEPTH) ** 0.25

V7X_SUBLANES = 8
V7X_MXU_DIM = 256
V7X_VMEM_BYTES = 64 * 1024 * 1024
V7X_VMEM_LIMIT_BYTES = 58 * 1024 * 1024
V7X_VMEM_FFN_LIMIT_BYTES = V7X_VMEM_BYTES - 512 * 1024

BF16 = jnp.bfloat16
F32 = jnp.float32


def _dot(a, b):
    return jnp.dot(a, b, preferred_element_type=F32)


def _layer_norm(x, g, b):
    mu = jnp.mean(x, axis=-1, keepdims=True)
    xc = x - mu
    var = jnp.mean(xc * xc, axis=-1, keepdims=True)
    return xc * lax.rsqrt(var + LN_EPS) * g + b


def _gelu_tanh(x):
    c = math.sqrt(2.0 / math.pi)
    return 0.5 * x * (1.0 + jnp.tanh(c * (x + 0.044715 * (x * x * x))))


def _softplus(x):
    return jnp.maximum(x, 0.0) + jnp.log1p(jnp.exp(-jnp.abs(x)))


def _params(*semantics, vmem_limit_bytes=V7X_VMEM_LIMIT_BYTES):
    return pltpu.CompilerParams(dimension_semantics=semantics, vmem_limit_bytes=vmem_limit_bytes)


_PROMPT_MOD_ROWS = V7X_SUBLANES


def _mod_spec(mod9, sub_layer, prompt):
    n_sample = mod9.shape[1] - _PROMPT_MOD_ROWS
    if prompt:
        return pl.BlockSpec((3, _PROMPT_MOD_ROWS, D_MODEL),
                            lambda i, *_: (sub_layer, n_sample // _PROMPT_MOD_ROWS, 0))
    return pl.BlockSpec((3, n_sample, D_MODEL), lambda i, *_: (sub_layer, 0, 0))


def _slabs(tm, mod9, prompt):
    n_sample = mod9.shape[1] - _PROMPT_MOD_ROWS
    if prompt:
        return [(0, tm)]
    return [(s, n_sample) for s in range(0, tm, n_sample)]


def _mod(mod_ref, c, row):
    return mod_ref[c] if row is None else mod_ref[c, pl.ds(row, 1), :]


def _modulated_bf16(x_ref, mod_ref, slabs, row):
    scale1, shift = 1.0 + _mod(mod_ref, 1, row), _mod(mod_ref, 0, row)
    parts = [(x_ref[r0:r0 + nr, :] * scale1 + shift).astype(BF16) for r0, nr in slabs]
    return parts[0] if len(parts) == 1 else jnp.concatenate(parts, axis=0)


def _mod_kernel(c_ref, w_ref, b_ref, o_ref):
    c = c_ref[...]
    a = (c * jax.nn.sigmoid(c)).astype(BF16)
    o_ref[0] = _dot(a, w_ref[...].astype(BF16)) + b_ref[...]


def _modulation(c_all, w_ada, b_ada):
    rows = c_all.shape[0]
    return pl.pallas_call(
        _mod_kernel,
        grid=(N_MOD,),
        in_specs=[
            pl.BlockSpec((rows, D_MODEL), lambda j: (0, 0)),
            pl.BlockSpec((D_MODEL, D_MODEL), lambda j: (0, j)),
            pl.BlockSpec((1, D_MODEL), lambda j: (0, j)),
        ],
        out_specs=pl.BlockSpec((1, rows, D_MODEL), lambda j: (j, 0, 0)),
        out_shape=jax.ShapeDtypeStruct((N_MOD, rows, D_MODEL), F32),
        compiler_params=_params("arbitrary"),
        name="adaln_mod",
    )(c_all, w_ada, b_ada)


def _ffn_kernel(x_ref, mod_ref, wg_ref, wv_ref, wd_ref, lng_ref, lnb_ref, *rest,
                slabs, n_f, emit_bf16, tiles_per_batch, n_cast, row_tile0, has_base):
    cast_in = rest[:n_cast]
    n_in = n_cast + (1 if has_base else 0)
    o_ref = rest[n_in]
    emit_out = rest[n_in + 1:n_in + 4] if emit_bf16 else ()
    cast_out = rest[n_in + 1 + len(emit_out):-1]
    u_ref = rest[-1]
    j = pl.program_id(1)
    row = (None if tiles_per_batch is None
           else (pl.program_id(0) + row_tile0) // tiles_per_batch)

    def swiglu_down():
        for src, dst in zip(cast_in, cast_out):
            dst[...] = src[...].astype(BF16)
        if emit_bf16:
            wg, wv, wd = (r[...].astype(BF16) for r in (wg_ref, wv_ref, wd_ref))
            for dst, w in zip(emit_out, (wg, wv, wd)):
                dst[...] = w
        else:
            wg, wv, wd = wg_ref[...], wv_ref[...], wd_ref[...]
        u = u_ref[...]
        g = _dot(u, wg)
        v = _dot(u, wv)
        h = (g * jax.nn.sigmoid(g) * v).astype(BF16)
        return _dot(h, wd)

    @pl.when(j == 0)
    def _():
        u_ref[...] = _modulated_bf16(x_ref, mod_ref, slabs, row)
        o_ref[...] = swiglu_down()

    @pl.when(jnp.logical_and(j > 0, j < n_f - 1))
    def _():
        o_ref[...] += swiglu_down()

    @pl.when(j == n_f - 1)
    def _():
        o_ref[...] += swiglu_down()
        for r0, nr in slabs:
            rows = pl.ds(r0, nr)
            y = ALPHA * x_ref[rows, :] + 0.5 * _mod(mod_ref, 2, row) * o_ref[rows, :]
            o_ref[rows, :] = _layer_norm(y, lng_ref[...], lnb_ref[...])


def _ffn(x, mod9, sub_layer, prompt_seq, w_g, w_v, v_col0, w_down, ln_g, ln_b, *, tm, tf,
         emit_bf16, cast_weights=(), row_tiles=None, base=None):
    rows = x.shape[0]
    prompt = prompt_seq is not None
    n_f = D_FF // tf
    v_blk0 = v_col0 // tf
    tile0, n_tiles = row_tiles if row_tiles is not None else (0, rows // tm)
    steps = n_tiles * n_f
    kern = functools.partial(_ffn_kernel, slabs=_slabs(tm, mod9, prompt), n_f=n_f,
                             emit_bf16=emit_bf16,
                             tiles_per_batch=prompt_seq // tm if prompt else None,
                             n_cast=len(cast_weights), row_tile0=tile0,
                             has_base=base is not None)
    up_spec = pl.BlockSpec((D_MODEL, tf), lambda i, j: (0, j))
    down_spec = pl.BlockSpec((tf, D_MODEL), lambda i, j: (j, 0))
    row_spec = pl.BlockSpec((tm, D_MODEL), lambda i, j: (i + tile0, 0))
    out_specs = [row_spec]
    out_shape = [jax.ShapeDtypeStruct((rows, D_MODEL), F32)]
    cast_specs = []
    for w, slab_rows in cast_weights:
        n_slabs = w.shape[0] // slab_rows
        assert n_slabs * slab_rows == w.shape[0] and n_slabs <= steps
        cast_specs.append(pl.BlockSpec(
            (slab_rows, w.shape[1]),
            lambda i, j, n_slabs=n_slabs: (jnp.minimum(i * n_f + j, n_slabs - 1), 0)))
    if emit_bf16:
        pad = tf if n_tiles > 1 else 0
        emit_blk = lambda i, j: jnp.where(i == 0, j, n_f)
        out_specs += [pl.BlockSpec((D_MODEL, tf), lambda i, j: (0, emit_blk(i, j)))] * 2
        out_specs += [pl.BlockSpec((tf, D_MODEL), lambda i, j: (emit_blk(i, j), 0))]
        out_shape += [jax.ShapeDtypeStruct((D_MODEL, D_FF + pad), BF16)] * 2
        out_shape += [jax.ShapeDtypeStruct((D_FF + pad, D_MODEL), BF16)]
    out_specs += cast_specs
    out_shape += [jax.ShapeDtypeStruct(w.shape, BF16) for w, _ in cast_weights]
    in_specs = [
        row_spec,
        _mod_spec(mod9, sub_layer, prompt),
        up_spec,
        pl.BlockSpec((D_MODEL, tf), lambda i, j: (0, v_blk0 + j)),
        down_spec,
        pl.BlockSpec((1, D_MODEL), lambda i, j: (0, 0)),
        pl.BlockSpec((1, D_MODEL), lambda i, j: (0, 0)),
    ] + cast_specs
    args = [x, mod9, w_g, w_v, w_down, ln_g, ln_b] + [w for w, _ in cast_weights]
    aliases = {}
    if base is not None:
        in_specs.append(pl.BlockSpec(memory_space=pl.ANY))
        args.append(base)
        aliases = {len(args) - 1: 0}
    return pl.pallas_call(
        kern,
        grid=(n_tiles, n_f),
        in_specs=in_specs,
        out_specs=out_specs,
        out_shape=out_shape,
        input_output_aliases=aliases,
        scratch_shapes=[pltpu.VMEM((tm, D_MODEL), BF16)],
        compiler_params=_params("arbitrary", "arbitrary", vmem_limit_bytes=V7X_VMEM_FFN_LIMIT_BYTES),
        name="macaron_ffn",
    )(*args)


def _lru_coeffs(xc, wbd_ref, ba, bx, lam):
    xcb = xc.astype(BF16)
    ra, rx = [], []
    for q in range(D_RNN // V7X_MXU_DIM):
        cols = slice(q * V7X_MXU_DIM, (q + 1) * V7X_MXU_DIM)
        rq = _dot(xcb[:, cols], wbd_ref[q])
        ra.append(rq[:, :V7X_MXU_DIM])
        rx.append(rq[:, V7X_MXU_DIM:])
    r = jax.nn.sigmoid(jnp.concatenate(ra, axis=-1) + ba)
    i = jax.nn.sigmoid(jnp.concatenate(rx, axis=-1) + bx)
    log_a = (-LRU_C * r) * _softplus(-lam)
    a = jnp.exp(log_a)
    z = -jnp.tanh(log_a) * (a * a + 1.0)
    mult = jnp.where(z > 0.0, z * lax.rsqrt(z), 0.0)
    return a, mult, i * xc


_MIXER_INPUTS = 13


def _prompt_mixer_kernel(*refs, tt, n_cast):
    (x_ref, mod_ref, w_ref, cw_ref, cb_ref, wbd_ref, ba_ref, bx_ref, lam_ref,
     glng_ref, glnb_ref, ws_ref, bs_ref) = refs[:_MIXER_INPUTS]
    cast_in = refs[_MIXER_INPUTS:_MIXER_INPUTS + n_cast]
    ya_ref, yb_ref, ht_ref, tailo_ref = refs[_MIXER_INPUTS + n_cast:_MIXER_INPUTS + n_cast + 4]
    cast_out = refs[_MIXER_INPUTS + n_cast + 4:_MIXER_INPUTS + 2 * n_cast + 4]
    tail_ref, h_ref, hs_ref = refs[_MIXER_INPUTS + 2 * n_cast + 4:]
    t = pl.program_id(1)
    ng = tt // V7X_SUBLANES

    for src, dst in zip(cast_in, cast_out):
        dst[...] = src[...].astype(BF16)

    @pl.when(t == 0)
    def _():
        tail_ref[...] = jnp.zeros_like(tail_ref)
        h_ref[...] = jnp.zeros_like(h_ref)

    u = _modulated_bf16(x_ref, mod_ref, [(0, tt)], pl.program_id(0))

    xr = _dot(u, w_ref[:, 0:D_RNN])
    gr = _dot(u, w_ref[:, D_RNN:2 * D_RNN])
    gu = _dot(u, w_ref[:, 2 * D_RNN:2 * D_RNN + D_GMLP])
    gv = _dot(u, w_ref[:, 2 * D_RNN + D_GMLP:BRANCH_COLS])

    prev = tail_ref[...]
    row = lax.broadcasted_iota(jnp.int32, (tt, D_RNN), 0)
    row8 = lax.broadcasted_iota(jnp.int32, (V7X_SUBLANES, D_RNN), 0)
    xc = cb_ref[...]
    for k in range(CONV_W):
        s = CONV_W - 1 - k
        if s == 0:
            xs = xr
        else:
            rolled = pltpu.roll(xr, s, 0)
            head = jnp.where(row8 < s, pltpu.roll(prev, s, 0), rolled[:V7X_SUBLANES])
            xs = jnp.concatenate([head, rolled[V7X_SUBLANES:]], axis=0)
        xc = xc + xs * cw_ref[k:k + 1, :]
    new_tail = xr[tt - V7X_SUBLANES:, :]
    tail_ref[...] = new_tail
    tailo_ref[0] = new_tail

    a, mult, gated = _lru_coeffs(xc, wbd_ref, ba_ref[...], bx_ref[...], lam_ref[...])
    mult = jnp.where(jnp.logical_and(row == 0, t == 0), 1.0, mult)
    b = mult * gated

    a = a.reshape(ng, V7X_SUBLANES, D_RNN)
    b = b.reshape(ng, V7X_SUBLANES, D_RNN)
    sub = lax.broadcasted_iota(jnp.int32, (ng, V7X_SUBLANES, D_RNN), 1)
    s = 1
    while s < V7X_SUBLANES:
        keep = sub >= s
        a_prev = jnp.where(keep, pltpu.roll(a, s, 1), 1.0)
        b_prev = jnp.where(keep, pltpu.roll(b, s, 1), 0.0)
        b = a * b_prev + b
        a = a * a_prev
        s *= 2
    h_in = h_ref[...]
    for g in range(ng):
        h = a[g] * h_in + b[g]
        hs_ref[g * V7X_SUBLANES:(g + 1) * V7X_SUBLANES, :] = h
        h_in = h[V7X_SUBLANES - 1:, :]
    h_ref[...] = h_in
    ht_ref[0] = h_in
    ya_ref[...] = (hs_ref[...] * _gelu_tanh(gr)).astype(BF16)

    vn = _layer_norm(gv, glng_ref[...], glnb_ref[...]).astype(BF16)
    ti = lax.broadcasted_iota(jnp.int32, (CHUNK, CHUNK), 0)
    si = lax.broadcasted_iota(jnp.int32, (CHUNK, CHUNK), 1)
    causal = si <= ti
    for g in range(GMLP_GROUPS):
        w = jnp.where(causal, ws_ref[g], 0.0).astype(BF16)
        c0 = g * GMLP_GW
        bias = bs_ref[:, c0:c0 + GMLP_GW]
        n_chunks = tt // CHUNK
        rhs = jnp.concatenate([vn[c * CHUNK:(c + 1) * CHUNK, c0:c0 + GMLP_GW]
                               for c in range(n_chunks)], axis=1)
        s_all = _dot(w, rhs)
        for c in range(n_chunks):
            r0 = c * CHUNK
            s_blk = s_all[:, c * GMLP_GW:(c + 1) * GMLP_GW] + bias
            yb_ref[r0:r0 + CHUNK, c0:c0 + GMLP_GW] = (
                gu[r0:r0 + CHUNK, c0:c0 + GMLP_GW] * s_blk).astype(BF16)


def _prompt_mixer(x, mod9, w_branch, conv_w, conv_b, wbd, ba, bx, lam, gln_g, gln_b, ws, bs_tile,
                  *, batch, seq, tt, cast_weights=()):
    nt = seq // tt
    steps = batch * nt
    kern = functools.partial(_prompt_mixer_kernel, tt=tt, n_cast=len(cast_weights))
    vec = pl.BlockSpec((1, D_RNN), lambda b, t: (0, 0))
    cast_in_specs = [pl.BlockSpec((w.shape[0] // steps, nc), lambda b, t, cb=cb: (b * nt + t, cb))
                     for w, nc, cb in cast_weights]
    cast_out_specs = [pl.BlockSpec((w.shape[0] // steps, nc), lambda b, t: (b * nt + t, 0))
                      for w, nc, cb in cast_weights]
    return pl.pallas_call(
        kern,
        grid=(batch, nt),
        in_specs=[
            pl.BlockSpec((tt, D_MODEL), lambda b, t: (b * nt + t, 0)),
            _mod_spec(mod9, 1, True),
            pl.BlockSpec((D_MODEL, BRANCH_COLS), lambda b, t: (0, 0)),
            pl.BlockSpec((CONV_W, D_RNN), lambda b, t: (0, 0)),
            vec,
            pl.BlockSpec((D_RNN // V7X_MXU_DIM, V7X_MXU_DIM, 2 * V7X_MXU_DIM), lambda b, t: (0, 0, 0)),
            vec, vec, vec, vec, vec,
            pl.BlockSpec((GMLP_GROUPS, CHUNK, CHUNK), lambda b, t: (0, 0, 0)),
            pl.BlockSpec((CHUNK, D_GMLP), lambda b, t: (0, 0)),
        ] + cast_in_specs,
        out_specs=[
            pl.BlockSpec((tt, D_RNN), lambda b, t: (b * nt + t, 0)),
            pl.BlockSpec((tt, D_GMLP), lambda b, t: (b * nt + t, 0)),
            pl.BlockSpec((1, 1, D_RNN), lambda b, t: (b, 0, 0)),
            pl.BlockSpec((1, V7X_SUBLANES, D_RNN), lambda b, t: (b, 0, 0)),
        ] + cast_out_specs,
        out_shape=[
            jax.ShapeDtypeStruct((batch * seq, D_RNN), BF16),
            jax.ShapeDtypeStruct((batch * seq, D_GMLP), BF16),
            jax.ShapeDtypeStruct((batch, 1, D_RNN), F32),
            jax.ShapeDtypeStruct((batch, V7X_SUBLANES, D_RNN), F32),
        ] + [jax.ShapeDtypeStruct((w.shape[0], nc), BF16) for w, nc, cb in cast_weights],
        scratch_shapes=[pltpu.VMEM((V7X_SUBLANES, D_RNN), F32), pltpu.VMEM((1, D_RNN), F32),
                        pltpu.VMEM((tt, D_RNN), F32)],
        compiler_params=_params("arbitrary", "arbitrary"),
        name="prompt_mixer",
    )(x, mod9, w_branch, conv_w, conv_b, wbd, ba, bx, lam, gln_g, gln_b, ws, bs_tile,
      *[w for w, nc, cb in cast_weights])


_PROJ_TN = 1024


def _sample_proj_kernel(x_ref, mod_ref, w_ref, lng_ref, lnb_ref, o_ref, wb_ref, u_ref, *, slabs):
    j = pl.program_id(0)

    @pl.when(j == 0)
    def _():
        u_ref[...] = _modulated_bf16(x_ref, mod_ref, slabs, None)

    wb_ref[...] = w_ref[...].astype(BF16)
    r = _dot(u_ref[...], wb_ref[...])

    @pl.when(jnp.logical_or(j == 0, j == 2))
    def _():
        o_ref[...] = r

    @pl.when(j == 1)
    def _():
        o_ref[...] = _gelu_tanh(r)

    @pl.when(j == 3)
    def _():
        o_ref[...] = _layer_norm(r, lng_ref[...], lnb_ref[...])


def _sample_proj(x, mod9, w_in, gln_g, gln_b):
    rows = x.shape[0]
    kern = functools.partial(_sample_proj_kernel, slabs=_slabs(rows, mod9, False))
    w_spec = pl.BlockSpec((D_MODEL, _PROJ_TN), lambda j: (0, j))
    return pl.pallas_call(
        kern,
        grid=(BRANCH_COLS // _PROJ_TN,),
        in_specs=[
            pl.BlockSpec((rows, D_MODEL), lambda j: (0, 0)),
            _mod_spec(mod9, 1, False),
            w_spec,
            pl.BlockSpec((1, D_GMLP), lambda j: (0, 0)),
            pl.BlockSpec((1, D_GMLP), lambda j: (0, 0)),
        ],
        out_specs=[pl.BlockSpec((rows, _PROJ_TN), lambda j: (0, j)), w_spec],
        out_shape=[jax.ShapeDtypeStruct((rows, BRANCH_COLS), F32),
                   jax.ShapeDtypeStruct((D_MODEL, BRANCH_COLS), BF16)],
        scratch_shapes=[pltpu.VMEM((rows, D_MODEL), BF16)],
        compiler_params=_params("arbitrary"),
        name="sample_proj",
    )(x, mod9, w_in, gln_g, gln_b)


def _sample_mix_kernel(xr_ref, gg_ref, gu_ref, vn_ref, sconv_ref, h0_ref, cw_ref, cb_ref,
                       wbd_ref, ba_ref, bx_ref, lam_ref, wsx_ref, bsx_ref,
                       ya_ref, yb_ref, ht_ref, xc_ref, *, nb, nt):
    def slab(t):
        return pl.ds(t * nb, nb)

    xpad = [sconv_ref[k] for k in range(CONV_W - 1)] + [xr_ref[slab(t), :] for t in range(nt)]
    for t in range(nt):
        xc = cb_ref[...]
        for k in range(CONV_W):
            xc = xc + xpad[t + k] * cw_ref[k:k + 1, :]
        xc_ref[slab(t), :] = xc

    a, mult, gated = _lru_coeffs(xc_ref[...], wbd_ref, ba_ref[...], bx_ref[...], lam_ref[...])
    b = mult * gated
    h = h0_ref[...]
    for t in range(nt):
        lo, hi = t * nb, (t + 1) * nb
        h = a[lo:hi] * h + b[lo:hi]
        ya_ref[slab(t), :] = (h * gg_ref[slab(t), :]).astype(BF16)
    ht_ref[...] = h

    for t in range(nt):
        s = bsx_ref[t:t + 1, :]
        for sp in range(t + 1):
            s = s + wsx_ref[t * nt + sp:t * nt + sp + 1, :] * vn_ref[slab(sp), :]
        yb_ref[slab(t), :] = (gu_ref[slab(t), :] * s).astype(BF16)


def _sample_mix(proj, sconv_tm, h0, conv_w, conv_b, wbd, ba, bx, lam, wsx, bsx, *, nb, nt):
    rows = nb * nt
    kern = functools.partial(_sample_mix_kernel, nb=nb, nt=nt)
    vec = pl.BlockSpec((1, D_RNN), lambda i: (0, 0))

    def col(jb):
        return pl.BlockSpec((rows, D_RNN), lambda i: (0, jb))

    return pl.pallas_call(
        kern,
        grid=(1,),
        in_specs=[
            col(0), col(1), col(2), col(3),
            pl.BlockSpec((CONV_W - 1, nb, D_RNN), lambda i: (0, 0, 0)),
            pl.BlockSpec((nb, D_RNN), lambda i: (0, 0)),
            pl.BlockSpec((CONV_W, D_RNN), lambda i: (0, 0)),
            vec,
            pl.BlockSpec((D_RNN // V7X_MXU_DIM, V7X_MXU_DIM, 2 * V7X_MXU_DIM), lambda i: (0, 0, 0)),
            vec, vec, vec,
            pl.BlockSpec((nt * nt, D_GMLP), lambda i: (0, 0)),
            pl.BlockSpec((nt, D_GMLP), lambda i: (0, 0)),
        ],
        out_specs=[
            pl.BlockSpec((rows, D_RNN), lambda i: (0, 0)),
            pl.BlockSpec((rows, D_GMLP), lambda i: (0, 0)),
            pl.BlockSpec((nb, D_RNN), lambda i: (0, 0)),
        ],
        out_shape=[
            jax.ShapeDtypeStruct((rows, D_RNN), BF16),
            jax.ShapeDtypeStruct((rows, D_GMLP), BF16),
            jax.ShapeDtypeStruct((nb, D_RNN), F32),
        ],
        scratch_shapes=[pltpu.VMEM((rows, D_RNN), F32)],
        compiler_params=_params("arbitrary"),
        name="sample_mix",
    )(proj, proj, proj, proj, sconv_tm, h0, conv_w, conv_b, wbd, ba, bx, lam, wsx, bsx)


def _merge_kernel(ya_ref, yb_ref, x_ref, mod_ref, wgate_ref, wpa_ref, wpb_ref, wo_ref,
                  lng_ref, lnb_ref, o_ref, *, slabs, tiles_per_batch):
    row = None if tiles_per_batch is None else pl.program_id(0) // tiles_per_batch
    u = _modulated_bf16(x_ref, mod_ref, slabs, row)
    gate = _mod(mod_ref, 2, row)
    y_a = jax.nn.sigmoid(_dot(u, wgate_ref[:, :D_MODEL])) * _dot(ya_ref[...], wpa_ref[...])
    y_b = jax.nn.sigmoid(_dot(u, wgate_ref[:, D_MODEL:])) * _dot(yb_ref[...], wpb_ref[...])
    mix = _dot((y_a + y_b).astype(BF16), wo_ref[...])
    for r0, nr in slabs:
        y = ALPHA * x_ref[r0:r0 + nr, :] + gate * mix[r0:r0 + nr, :]
        o_ref[r0:r0 + nr, :] = _layer_norm(y, lng_ref[...], lnb_ref[...])


def _merge(ya, yb, x, mod9, prompt_seq, w_gate, w_pa, w_pb, w_out, ln_g, ln_b, *, tm):
    rows = x.shape[0]
    prompt = prompt_seq is not None
    kern = functools.partial(_merge_kernel, slabs=_slabs(tm, mod9, prompt),
                             tiles_per_batch=prompt_seq // tm if prompt else None)
    whole = lambda shape: pl.BlockSpec(shape, lambda i: (0, 0))
    in_specs = [
        pl.BlockSpec((tm, D_RNN), lambda i: (i, 0)),
        pl.BlockSpec((tm, D_GMLP), lambda i: (i, 0)),
        pl.BlockSpec((tm, D_MODEL), lambda i: (i, 0)),
        _mod_spec(mod9, 1, prompt),
        whole((D_MODEL, 2 * D_MODEL)),
        whole((D_RNN, D_MODEL)),
        whole((D_GMLP, D_MODEL)),
        whole((D_MODEL, D_MODEL)),
        whole((1, D_MODEL)),
        whole((1, D_MODEL)),
    ]
    return pl.pallas_call(
        kern,
        grid=(rows // tm,),
        in_specs=in_specs,
        out_specs=pl.BlockSpec((tm, D_MODEL), lambda i: (i, 0)),
        out_shape=jax.ShapeDtypeStruct((rows, D_MODEL), F32),
        compiler_params=_params("arbitrary", vmem_limit_bytes=V7X_VMEM_FFN_LIMIT_BYTES),
        name="merge_out",
    )(ya, yb, x, mod9, w_gate, w_pa, w_pb, w_out, ln_g, ln_b)


def _block_diag_tiles(w):
    per = V7X_MXU_DIM // RNN_BW
    w4 = w.reshape(RNN_BLOCKS // per, per, RNN_BW, RNN_BW)
    eye = jnp.eye(per, dtype=w.dtype)
    t = w4[:, :, :, None, :] * eye[None, :, None, :, None]
    return t.reshape(RNN_BLOCKS // per, V7X_MXU_DIM, V7X_MXU_DIM)


def kernel(x_prompt, x_sample, state_conv, state_h, c_prompt, c_sample, w_ada, b_ada, ffn1_w_gu, ffn1_w_down, ffn2_w_gu, ffn2_w_down, w_in, conv_w, conv_b, lru_wa, lru_ba, lru_wx, lru_bx, lru_lambda, gmlp_ln_g, gmlp_ln_b, gmlp_ws, gmlp_bs, w_pa, w_pb, w_out, ln_g, ln_b):
    assert w_ada.shape[0] == DEPTH == 1
    bp, seq, _ = x_prompt.shape
    bs, dseq, _ = x_sample.shape
    l = 0

    wbd = jnp.concatenate([_block_diag_tiles(lru_wa[l]), _block_diag_tiles(lru_wx[l])],
                          axis=-1).astype(BF16)

    row = lambda v: v.reshape(1, -1)
    cb, ba, bx, lam = row(conv_b[l]), row(lru_ba[l]), row(lru_bx[l]), row(lru_lambda[l])
    gln_g, gln_b = row(gmlp_ln_g[l]), row(gmlp_ln_b[l])
    lng = [row(ln_g[l, k]) for k in range(3)]
    lnb = [row(ln_b[l, k]) for k in range(3)]

    bs_tile = jnp.repeat(gmlp_bs[l].T, GMLP_GW, axis=1)
    wsx = jnp.repeat(jnp.transpose(gmlp_ws[l][:, :dseq, :dseq], (1, 2, 0)), GMLP_GW, axis=2)
    wsx = wsx.reshape(dseq * dseq, D_GMLP)
    bsx = bs_tile[:dseq]

    assert bs % V7X_SUBLANES == 0 and bp <= _PROMPT_MOD_ROWS
    c_all = jnp.concatenate(
        [c_sample, c_prompt, jnp.zeros((_PROMPT_MOD_ROWS - bp, D_MODEL), c_prompt.dtype)], axis=0)
    mod9 = _modulation(c_all, w_ada[l], row(b_ada[l]))

    xp = x_prompt.reshape(bp * seq, D_MODEL)
    xs = jnp.transpose(x_sample, (1, 0, 2)).reshape(dseq * bs, D_MODEL)
    n_s = dseq * bs
    keep = CONV_W - 1

    def ffn(x, prompt_seq, k, w_g, w_v, v_col0, w_down, tm, tf, **kw):
        kw.setdefault("emit_bf16", False)
        return _ffn(x, mod9, k, prompt_seq, w_g, w_v, v_col0, w_down, lng[k], lnb[k],
                    tm=tm, tf=tf, **kw)

    def merge(ya, yb, x1, prompt_seq, merge_w, tm):
        return _merge(ya, yb, x1, mod9, prompt_seq, *merge_w, lng[1], lnb[1], tm=tm)

    x1p_head, w_g_b, w_v_b, w_d_b = ffn(xp, seq, 0, ffn1_w_gu[l], ffn1_w_gu[l], D_FF,
                                        ffn1_w_down[l], 1024, 512, emit_bf16=True,
                                        row_tiles=(0, 1))
    x1p, w_gu2_b, w_dn2_b = ffn(xp, seq, 0, w_g_b, w_v_b, 0, w_d_b, 1024, 512,
                                cast_weights=((ffn2_w_gu[l], 32), (ffn2_w_down[l], 128)),
                                row_tiles=(1, bp * seq // 1024 - 1), base=x1p_head)
    (x1s,) = ffn(xs, None, 0, w_g_b, w_v_b, 0, w_d_b, n_s, 1408)

    proj_s, w_branch = _sample_proj(x1s, mod9, w_in[l], gln_g, gln_b)
    sconv_tm = jnp.transpose(state_conv[l], (1, 0, 2))
    ya_s, yb_s, ht_s = _sample_mix(proj_s, sconv_tm, state_h[l], conv_w[l], cb, wbd, ba, bx, lam,
                                   wsx, bsx, nb=bs, nt=dseq)
    ya_p, yb_p, ht_p, tail_p, *merge_w = _prompt_mixer(
        x1p, mod9, w_branch, conv_w[l], cb, wbd, ba, bx, lam, gln_g, gln_b,
        gmlp_ws[l], bs_tile, batch=bp, seq=seq, tt=512,
        cast_weights=((w_in[l], 2 * D_MODEL, 1), (w_pa[l], D_MODEL, 0), (w_pb[l], D_MODEL, 0),
                      (w_out[l], D_MODEL, 0)))
    x2s = merge(ya_s, yb_s, x1s, None, merge_w, 256)
    x2p = merge(ya_p, yb_p, x1p, seq, merge_w, 512)

    (x3p,) = ffn(x2p, seq, 2, w_gu2_b, w_gu2_b, D_FF, w_dn2_b, 1024, 512)
    (x3s,) = ffn(x2s, None, 2, w_gu2_b, w_gu2_b, D_FF, w_dn2_b, n_s, 1408)

    y_prompt = x3p.reshape(bp, seq, D_MODEL)
    y_sample = jnp.transpose(x3s.reshape(dseq, bs, D_MODEL), (1, 0, 2))
    new_conv_p = tail_p[:, V7X_SUBLANES - keep:][None]
    new_h_p = ht_p.reshape(bp, D_RNN)[None]
    xr_s = lax.slice(proj_s.reshape(dseq, bs, BRANCH_COLS), (dseq - keep, 0, 0), (dseq, bs, D_RNN))
    new_conv_s = jnp.transpose(xr_s, (1, 0, 2))[None]
    new_h_s = ht_s[None]
    vn_s = lax.slice(proj_s.reshape(dseq, bs, BRANCH_COLS), (0, 0, 2 * D_RNN + D_GMLP),
                     (dseq, bs, BRANCH_COLS))
    new_v_s = jnp.transpose(vn_s, (1, 0, 2))[None]
    return (y_prompt, y_sample, new_conv_p, new_h_p, new_conv_s, new_h_s, new_v_s)
```
